```python
import functools
import jax, jax.numpy as jnp
from jax import lax
import numpy as np

D_MODEL = 1024
BATCH = 16
SEQ = 2048
DEPTH = 2
DEC_BATCH = 32
DEC_SEQ = 4
PAST_LEN = 16384
PAGE_SIZE = 128

N_HEADS = 8
HEAD_DIM = 64
ATTN_WIDTH = N_HEADS * HEAD_DIM
MOBA_BLOCK = 256
MOBA_TOPK = 3
Q_CHUNK = 16
POOL_WINDOWS = (2, 4, 8, 16)
POOL_GROUPS = len(POOL_WINDOWS)
POOL_WIDTH = D_MODEL // 2
POOL_GROUP_WIDTH = POOL_WIDTH // POOL_GROUPS
POOL_HIST = max(POOL_WINDOWS) - 1
D_FF = 4 * D_MODEL
IN_WIDTH = POOL_WIDTH + 3 * ATTN_WIDTH + 2 * D_MODEL
IN_SPLITS = (POOL_WIDTH, POOL_WIDTH + ATTN_WIDTH, POOL_WIDTH + 2 * ATTN_WIDTH,
             POOL_WIDTH + 3 * ATTN_WIDTH, POOL_WIDTH + 3 * ATTN_WIDTH + D_MODEL)
ADA_CHUNKS = 6
EPS = 1e-6
NEG_INF = -1e30

kernel_name = 'hybrid_pool_moba_decoder_step'


def rmsnorm(x, g):
    xf = x.astype(jnp.float32)
    y = xf * lax.rsqrt(jnp.mean(xf * xf, axis=-1, keepdims=True) + EPS)
    return (y * g.astype(jnp.float32)).astype(x.dtype)


def pool_mixer(u, hist, pos0, lin, scale):
    B, L, _ = u.shape
    z = jnp.concatenate([hist.astype(u.dtype), u], axis=1)
    cs = jnp.cumsum(z.astype(jnp.float32), axis=1)
    cs = jnp.concatenate([jnp.zeros((B, 1, POOL_WIDTH), jnp.float32), cs], axis=1)
    pos = pos0 + jnp.arange(L, dtype=jnp.int32)
    end = cs[:, POOL_HIST + 1:POOL_HIST + 1 + L]
    means = []
    for g, w in enumerate(POOL_WINDOWS):
        cols = slice(g * POOL_GROUP_WIDTH, (g + 1) * POOL_GROUP_WIDTH)
        win = end[..., cols] - cs[:, POOL_HIST + 1 - w:POOL_HIST + 1 - w + L, cols]
        cnt = jnp.minimum(pos + 1, w).astype(jnp.float32)[None, :, None]
        means.append(win / cnt)
    d = jnp.concatenate(means, axis=-1) - u.astype(jnp.float32)
    d = d.astype(u.dtype).reshape(B, L, POOL_GROUPS, POOL_GROUP_WIDTH)
    mixed = jnp.einsum('blgc,gcd->blgd', d, lin).reshape(B, L, POOL_WIDTH)
    return mixed * scale, z[:, -POOL_HIST:]


def select_blocks(q, kmean, n_past):
    nc = kmean.shape[2]
    s = jnp.einsum('bhqd,bhnd->bhqn', q, kmean).astype(jnp.float32)
    ok = jnp.arange(nc)[None, :] < n_past[:, None]
    s = jnp.where(ok, s, NEG_INF)
    _, idx = lax.top_k(s, MOBA_TOPK)
    valid = idx < n_past[:, None]
    return idx, valid


def moba_attend(q, k_sel, v_sel, sel_ok, k_own, v_own, own_ok):
    ns = k_sel.shape[3]
    sc = HEAD_DIM ** -0.5
    s_sel = jnp.einsum('bhqd,bhqnd->bhqn', q, k_sel).astype(jnp.float32) * sc
    s_own = jnp.einsum('bhqd,bhkd->bhqk', q, k_own).astype(jnp.float32) * sc
    s = jnp.concatenate([jnp.where(sel_ok, s_sel, NEG_INF), jnp.where(own_ok, s_own, NEG_INF)], axis=-1)
    p = jax.nn.softmax(s, axis=-1).astype(v_sel.dtype)
    return (jnp.einsum('bhqn,bhqnd->bhqd', p[..., :ns], v_sel)
            + jnp.einsum('bhqk,bhkd->bhqd', p[..., ns:], v_own))


def moba_prompt(q, k, v):
    B, S = q.shape[0], q.shape[1]
    nb = -(-S // MOBA_BLOCK)
    nc = max(nb, MOBA_TOPK)
    pad = nb * MOBA_BLOCK - S
    qh = q.transpose(0, 2, 1, 3)
    padw = ((0, 0), (0, 0), (0, pad), (0, 0))
    kb = jnp.pad(k.transpose(0, 2, 1, 3), padw).reshape(B, N_HEADS, nb, MOBA_BLOCK, HEAD_DIM)
    vb = jnp.pad(v.transpose(0, 2, 1, 3), padw).reshape(B, N_HEADS, nb, MOBA_BLOCK, HEAD_DIM)
    kmean = jnp.mean(kb.astype(jnp.float32), axis=3)
    kmean = jnp.pad(kmean, ((0, 0), (0, 0), (0, nc - nb), (0, 0))).astype(q.dtype)
    pos = jnp.arange(S, dtype=jnp.int32)
    idx, valid = select_blocks(qh, kmean, pos // MOBA_BLOCK)
    idx = jnp.minimum(idx, nb - 1)
    bi = jnp.arange(B)[:, None, None, None]
    hi = jnp.arange(N_HEADS)[None, :, None, None]

    def chunk(ci):
        s0 = ci * Q_CHUNK
        qc = lax.dynamic_slice_in_dim(qh, s0, Q_CHUNK, axis=2)
        ic = lax.dynamic_slice_in_dim(idx, s0, Q_CHUNK, axis=2)
        okc = lax.dynamic_slice_in_dim(valid, s0, Q_CHUNK, axis=2)
        k_sel = kb[bi, hi, ic].reshape(B, N_HEADS, Q_CHUNK, MOBA_TOPK * MOBA_BLOCK, HEAD_DIM)
        v_sel = vb[bi, hi, ic].reshape(B, N_HEADS, Q_CHUNK, MOBA_TOPK * MOBA_BLOCK, HEAD_DIM)
        sel_ok = jnp.repeat(okc, MOBA_BLOCK, axis=-1)
        blk = s0 // MOBA_BLOCK
        k_own = lax.dynamic_index_in_dim(kb, blk, axis=2, keepdims=False)
        v_own = lax.dynamic_index_in_dim(vb, blk, axis=2, keepdims=False)
        qpos = s0 + jnp.arange(Q_CHUNK)
        kpos = blk * MOBA_BLOCK + jnp.arange(MOBA_BLOCK)
        own_ok = kpos[None, :] <= qpos[:, None]
        return moba_attend(qc, k_sel, v_sel, sel_ok, k_own, v_own, own_ok)

    out = lax.map(chunk, jnp.arange(S // Q_CHUNK))
    return out.transpose(1, 0, 3, 2, 4).reshape(B, S, ATTN_WIDTH)


def moba_sample(q, k, v, *, cache_k, cache_v, page_table, layer):
    Bd, T = q.shape[0], q.shape[1]
    n_pages = PAST_LEN // PAGE_SIZE
    ppb = MOBA_BLOCK // PAGE_SIZE
    b0 = PAST_LEN // MOBA_BLOCK
    nc = max(b0, MOBA_TOPK)
    kp = cache_k[layer, page_table[:, :b0 * ppb]]
    kmean = jnp.mean(kp.astype(jnp.float32).reshape(Bd, b0, MOBA_BLOCK, N_HEADS, HEAD_DIM), axis=2)
    kmean = jnp.pad(kmean.transpose(0, 2, 1, 3), ((0, 0), (0, 0), (0, nc - b0), (0, 0))).astype(q.dtype)
    qh = q.transpose(0, 2, 1, 3)
    idx, valid = select_blocks(qh, kmean, jnp.full((T,), b0, jnp.int32))
    lp = jnp.minimum(idx[..., None] * ppb + jnp.arange(ppb), n_pages - 1)
    bi = jnp.arange(Bd)[:, None, None, None, None]
    hi = jnp.arange(N_HEADS)[None, :, None, None, None]
    phys = page_table[bi, lp]
    k_sel = cache_k[layer, phys, :, hi].reshape(Bd, N_HEADS, T, MOBA_TOPK * MOBA_BLOCK, HEAD_DIM)
    v_sel = cache_v[layer, phys, :, hi].reshape(Bd, N_HEADS, T, MOBA_TOPK * MOBA_BLOCK, HEAD_DIM)
    sel_ok = jnp.repeat(valid, MOBA_BLOCK, axis=-1)
    r_pages = (PAST_LEN - b0 * MOBA_BLOCK) // PAGE_SIZE
    r = r_pages * PAGE_SIZE
    own_phys = page_table[:, b0 * ppb:b0 * ppb + r_pages]
    k_own_past = cache_k[layer, own_phys].reshape(Bd, r, N_HEADS, HEAD_DIM)
    v_own_past = cache_v[layer, own_phys].reshape(Bd, r, N_HEADS, HEAD_DIM)
    k_own = jnp.concatenate([k_own_past.astype(k.dtype), k], axis=1).transpose(0, 2, 1, 3)
    v_own = jnp.concatenate([v_own_past.astype(v.dtype), v], axis=1).transpose(0, 2, 1, 3)
    own_ok = jnp.arange(r + T)[None, :] <= (r + jnp.arange(T))[:, None]
    out = moba_attend(qh, k_sel, v_sel, sel_ok, k_own, v_own, own_ok)
    return out.transpose(0, 2, 1, 3).reshape(Bd, T, ATTN_WIDTH)


def trunk_layer(x, c, pool_hist, pos0, attn_fn, w_ada, b_ada, g_mix_pre, g_mix_post, w_in, pool_lin,
                pool_scale, w_branch_pool, w_branch_attn, w_out, g_ffn_pre, g_ffn_post, w_ff1, w_ff2):
    B, L, _ = x.shape
    ada = (c @ w_ada + b_ada)[:, None, :]
    shift1, scale1, gate1, shift2, scale2, gate2 = jnp.split(ada, ADA_CHUNKS, axis=-1)
    h = rmsnorm(x, g_mix_pre) * (1 + scale1) + shift1
    proj = h @ w_in
    u, q, k, v, gp, ga = jnp.split(proj, IN_SPLITS, axis=-1)
    pool_out, new_hist = pool_mixer(u, pool_hist, pos0, pool_lin, pool_scale)
    q = q.reshape(B, L, N_HEADS, HEAD_DIM)
    k = k.reshape(B, L, N_HEADS, HEAD_DIM)
    v = v.reshape(B, L, N_HEADS, HEAD_DIM)
    attn_out = attn_fn(q, k, v)
    merged = jax.nn.sigmoid(gp) * (pool_out @ w_branch_pool) + jax.nn.sigmoid(ga) * (attn_out @ w_branch_attn)
    x = x + gate1 * rmsnorm(merged @ w_out, g_mix_post)
    h2 = rmsnorm(x, g_ffn_pre) * (1 + scale2) + shift2
    ff = jnp.square(jax.nn.relu(h2 @ w_ff1)) @ w_ff2
    x = x + gate2 * rmsnorm(ff, g_ffn_post)
    return x, k, v, new_hist


def setup_inputs(seed: int = 0) -> dict:
    key = jax.random.key(seed)
    ks = jax.random.split(key, 24)
    n_pages = PAST_LEN // PAGE_SIZE
    n_phys = (DEC_BATCH * n_pages * 5) // 4
    f32 = jnp.float32

    def nrm(k, shape, s):
        return jax.random.normal(k, shape, f32) * s

    perm = jax.random.permutation(ks[5], n_phys)[:DEC_BATCH * n_pages]
    page_table = perm.reshape(DEC_BATCH, n_pages).astype(jnp.int32)
    return {
        'x_prompt': nrm(ks[0], (BATCH, SEQ, D_MODEL), 1.0),
        'x_sample': nrm(ks[1], (DEC_BATCH, DEC_SEQ, D_MODEL), 1.0),
        'cache_k': nrm(ks[2], (DEPTH, n_phys, PAGE_SIZE, N_HEADS, HEAD_DIM), 1.0),
        'cache_v': nrm(ks[3], (DEPTH, n_phys, PAGE_SIZE, N_HEADS, HEAD_DIM), 1.0),
        'state_pool': nrm(ks[4], (DEPTH, DEC_BATCH, POOL_HIST, POOL_WIDTH), 1.0),
        'page_table': page_table,
        'c_prompt': nrm(ks[6], (BATCH, D_MODEL), 1.0),
        'c_sample': nrm(ks[7], (DEC_BATCH, D_MODEL), 1.0),
        'w_ada': nrm(ks[8], (DEPTH, D_MODEL, ADA_CHUNKS * D_MODEL), 0.5 * D_MODEL ** -0.5),
        'b_ada': nrm(ks[9], (DEPTH, ADA_CHUNKS * D_MODEL), 0.01),
        'g_mix_pre': 1.0 + nrm(ks[10], (DEPTH, D_MODEL), 0.05),
        'g_mix_post': 1.0 + nrm(ks[11], (DEPTH, D_MODEL), 0.05),
        'w_in': nrm(ks[12], (DEPTH, D_MODEL, IN_WIDTH), D_MODEL ** -0.5),
        'pool_lin': nrm(ks[13], (DEPTH, POOL_GROUPS, POOL_GROUP_WIDTH, POOL_GROUP_WIDTH), POOL_GROUP_WIDTH ** -0.5),
        'pool_scale': 1.0 + nrm(ks[14], (DEPTH, POOL_WIDTH), 0.05),
        'w_branch_pool': nrm(ks[15], (DEPTH, POOL_WIDTH, D_MODEL), POOL_WIDTH ** -0.5),
        'w_branch_attn': nrm(ks[16], (DEPTH, ATTN_WIDTH, D_MODEL), ATTN_WIDTH ** -0.5),
        'w_out': nrm(ks[17], (DEPTH, D_MODEL, D_MODEL), D_MODEL ** -0.5),
        'g_ffn_pre': 1.0 + nrm(ks[18], (DEPTH, D_MODEL), 0.05),
        'g_ffn_post': 1.0 + nrm(ks[19], (DEPTH, D_MODEL), 0.05),
        'w_ff1': nrm(ks[20], (DEPTH, D_MODEL, D_FF), D_MODEL ** -0.5),
        'w_ff2': nrm(ks[21], (DEPTH, D_FF, D_MODEL), D_FF ** -0.5),
    }


def reference(x_prompt, x_sample, cache_k, cache_v, state_pool, page_table, c_prompt, c_sample,
              w_ada, b_ada, g_mix_pre, g_mix_post, w_in, pool_lin, pool_scale, w_branch_pool,
              w_branch_attn, w_out, g_ffn_pre, g_ffn_post, w_ff1, w_ff2):
    zero_hist = jnp.zeros((x_prompt.shape[0], POOL_HIST, POOL_WIDTH), x_prompt.dtype)
    y_p, y_s = x_prompt, x_sample
    kp_l, vp_l, hp_l, ks_l, vs_l, hs_l = [], [], [], [], [], []
    for l in range(DEPTH):
        lw = (w_ada[l], b_ada[l], g_mix_pre[l], g_mix_post[l], w_in[l], pool_lin[l], pool_scale[l],
              w_branch_pool[l], w_branch_attn[l], w_out[l], g_ffn_pre[l], g_ffn_post[l], w_ff1[l], w_ff2[l])
        y_p, k_p, v_p, h_p = trunk_layer(y_p, c_prompt, zero_hist, 0, moba_prompt, *lw)
        sample_attn = functools.partial(moba_sample, cache_k=cache_k, cache_v=cache_v,
                                        page_table=page_table, layer=l)
        y_s, k_s, v_s, h_s = trunk_layer(y_s, c_sample, state_pool[l], PAST_LEN, sample_attn, *lw)
        kp_l.append(k_p); vp_l.append(v_p); hp_l.append(h_p)
        ks_l.append(k_s); vs_l.append(v_s); hs_l.append(h_s)
    k_prompt = jnp.stack(kp_l)
    v_prompt = jnp.stack(vp_l)
    pool_prompt = jnp.stack(hp_l)
    k_sample = jnp.stack(ks_l)
    v_sample = jnp.stack(vs_l)
    pool_sample = jnp.stack(hs_l)
    return (y_p, y_s, k_prompt, v_prompt, pool_prompt, k_sample, v_sample, pool_sample)
```

```python
import functools

import jax
import jax.numpy as jnp
from jax import lax
from jax.experimental import pallas as pl
from jax.experimental.pallas import tpu as pltpu

F32 = jnp.float32
BF16 = jnp.bfloat16

D_MODEL = 1024
N_HEADS = 8
HEAD_DIM = 64
ATTN_WIDTH = N_HEADS * HEAD_DIM
MOBA_BLOCK = 256
MOBA_TOPK = 3
PAGE_SIZE = 128
PAGES_PER_BLOCK = MOBA_BLOCK // PAGE_SIZE
POOL_WINDOWS = (2, 4, 8, 16)
POOL_WIDTH = 512
POOL_GROUP_WIDTH = 128
POOL_HIST = 15
HIST_PAD = 16
D_FF = 4 * D_MODEL
ADA_CHUNKS = 6
EPS = 1e-6
NEG_INF = -1e30
SM_SCALE = HEAD_DIM ** -0.5

V7X_VMEM_LIMIT = 56 * 1024 * 1024

TOKEN_TILE = 256
POOL_TILE = 512
KMEAN_CHUNK_PAGES = 8


def _cparams(*sem):
    return pltpu.CompilerParams(dimension_semantics=sem, vmem_limit_bytes=V7X_VMEM_LIMIT)


def _rms(x, g):
    return x * lax.rsqrt(jnp.mean(x * x, axis=-1, keepdims=True) + EPS) * g


def _sigmoid(x):
    return 1.0 / (1.0 + jnp.exp(-x))


def _bdot(a, b):
    return jnp.dot(a, b, preferred_element_type=F32)


def _ada_kernel(c_ref, w_ref, b_ref, o_ref):
    o_ref[0] = _bdot(c_ref[...].astype(BF16), w_ref[0].astype(BF16)) + b_ref[0]


def _ada(c_all, w_ada, b_ada):
    depth, _, width = w_ada.shape
    n = c_all.shape[0]
    tn = 1536
    return pl.pallas_call(
        _ada_kernel,
        grid=(depth, width // tn),
        in_specs=[pl.BlockSpec((n, D_MODEL), lambda l, j: (0, 0)),
                  pl.BlockSpec((1, D_MODEL, tn), lambda l, j: (l, 0, j)),
                  pl.BlockSpec((1, 1, tn), lambda l, j: (l, 0, j))],
        out_specs=pl.BlockSpec((1, n, tn), lambda l, j: (l, 0, j)),
        out_shape=jax.ShapeDtypeStruct((depth, n, width), F32),
        compiler_params=_cparams("arbitrary", "arbitrary"),
        name="ada",
    )(c_all, w_ada, b_ada.reshape(depth, 1, width))


def _mod_spec(mod, chunk, tiles_per_group):
    rows = mod.shape[1]
    return pl.BlockSpec((1, rows, D_MODEL), lambda i: (i // tiles_per_group, 0, chunk))


def _inproj_kernel(x_ref, sc_ref, sh_ref, g_ref, wu_ref, wqkv_ref, wg_ref,
                   u_ref, qT_ref, kT_ref, vT_ref, gp_ref, ga_ref):
    h = (_rms(x_ref[...], g_ref[...]) * (1.0 + sc_ref[0]) + sh_ref[0]).astype(BF16)
    u_ref[...] = _bdot(h, wu_ref[...])
    qkvT = lax.dot_general(wqkv_ref[...], h, (((1,), (1,)), ((), ())), preferred_element_type=F32)
    qT_ref[0] = qkvT[0:ATTN_WIDTH]
    kT_ref[0] = qkvT[ATTN_WIDTH:2 * ATTN_WIDTH]
    vT_ref[0] = qkvT[2 * ATTN_WIDTH:3 * ATTN_WIDTH]
    gp_ref[...] = _bdot(h, wg_ref[:, 0:D_MODEL])
    ga_ref[...] = _bdot(h, wg_ref[:, D_MODEL:2 * D_MODEL])


def _inproj(x, mod, g, wu, wqkvT, wg, *, tm, seq):
    n_tok = x.shape[0]
    tpg = seq // tm
    n_seq = n_tok // seq
    const = lambda i: (0, 0)
    tok = lambda w: pl.BlockSpec((tm, w), lambda i: (i, 0))
    tspec = pl.BlockSpec((1, ATTN_WIDTH, tm), lambda i: (i // tpg, 0, i % tpg))
    tshape = jax.ShapeDtypeStruct((n_seq, ATTN_WIDTH, seq), F32)
    return pl.pallas_call(
        _inproj_kernel,
        grid=(n_tok // tm,),
        in_specs=[tok(D_MODEL), _mod_spec(mod, 1, tpg), _mod_spec(mod, 0, tpg),
                  pl.BlockSpec((1, D_MODEL), const),
                  pl.BlockSpec(wu.shape, const), pl.BlockSpec(wqkvT.shape, const), pl.BlockSpec(wg.shape, const)],
        out_specs=[tok(POOL_WIDTH), tspec, tspec, tspec, tok(D_MODEL), tok(D_MODEL)],
        out_shape=[jax.ShapeDtypeStruct((n_tok, POOL_WIDTH), F32), tshape, tshape, tshape,
                   jax.ShapeDtypeStruct((n_tok, D_MODEL), F32), jax.ShapeDtypeStruct((n_tok, D_MODEL), F32)],
        compiler_params=_cparams("arbitrary"),
        name="inproj",
    )(x, mod, mod, g, wu, wqkvT, wg)


def _pool_prompt_kernel(u_ref, d_ref, z_ref):
    tm = u_ref.shape[1]
    t = pl.program_id(1)

    @pl.when(t == 0)
    def _():
        z_ref[0:HIST_PAD, :] = jnp.zeros((HIST_PAD, POOL_WIDTH), F32)

    @pl.when(t > 0)
    def _():
        z_ref[0:HIST_PAD, :] = z_ref[tm:tm + HIST_PAD, :]

    z_ref[HIST_PAD:HIST_PAD + tm, :] = u_ref[0]
    pos = t * tm + lax.broadcasted_iota(jnp.int32, (tm, POOL_GROUP_WIDTH), 0)
    for g, w in enumerate(POOL_WINDOWS):
        cols = slice(g * POOL_GROUP_WIDTH, (g + 1) * POOL_GROUP_WIDTH)
        cur = z_ref[HIST_PAD:HIST_PAD + tm, cols]
        acc = cur
        for r in range(1, w):
            acc = acc + z_ref[HIST_PAD - r:HIST_PAD - r + tm, cols]
        cnt = jnp.minimum(pos + 1, w).astype(F32)
        d_ref[0, :, cols] = (acc / cnt - cur).astype(BF16)


def _pool_prompt(u, *, tm):
    b, s, _ = u.shape
    return pl.pallas_call(
        _pool_prompt_kernel,
        grid=(b, s // tm),
        in_specs=[pl.BlockSpec((1, tm, POOL_WIDTH), lambda i, t: (i, t, 0))],
        out_specs=pl.BlockSpec((1, tm, POOL_WIDTH), lambda i, t: (i, t, 0)),
        out_shape=jax.ShapeDtypeStruct((b, s, POOL_WIDTH), BF16),
        scratch_shapes=[pltpu.VMEM((HIST_PAD + tm, POOL_WIDTH), F32)],
        compiler_params=_cparams("arbitrary", "arbitrary"),
        name="pool_prompt",
    )(u)


def _pool_sample_kernel(z_ref, d_ref):
    n_new = d_ref.shape[0]
    for t in range(n_new):
        for g, w in enumerate(POOL_WINDOWS):
            cols = slice(g * POOL_GROUP_WIDTH, (g + 1) * POOL_GROUP_WIDTH)
            cur = z_ref[POOL_HIST + t, :, cols]
            acc = cur
            for r in range(1, w):
                acc = acc + z_ref[POOL_HIST + t - r, :, cols]
            d_ref[t, :, cols] = (acc / float(w) - cur).astype(BF16)


def _pool_sample(z_tm, n_new):
    _, bd, _ = z_tm.shape
    return pl.pallas_call(
        _pool_sample_kernel,
        out_shape=jax.ShapeDtypeStruct((n_new, bd, POOL_WIDTH), BF16),
        name="pool_sample",
    )(z_tm)


def _moba_kernel(qT_ref, kT_ref, vT_ref, o_ref, kb_ref, vb_ref, qm_ref, sel_ref):
    hp = 2 * HEAD_DIM
    s_len = qT_ref.shape[2]
    nb = s_len // MOBA_BLOCK
    qT = qT_ref[0]
    k = kT_ref[0].T
    kmean = jnp.sum(k.reshape(nb, MOBA_BLOCK, hp), axis=1) * (1.0 / MOBA_BLOCK)
    kb_ref[...] = k.astype(BF16).reshape(nb, MOBA_BLOCK, hp)
    for j in range(nb):
        vb_ref[j] = vT_ref[0, :, j * MOBA_BLOCK:(j + 1) * MOBA_BLOCK].astype(BF16)

    row_head = lax.broadcasted_iota(jnp.int32, (hp, s_len), 0) // HEAD_DIM
    lane_head = lax.broadcasted_iota(jnp.int32, (nb, hp), 1) // HEAD_DIM
    blk = lax.broadcasted_iota(jnp.int32, (nb, s_len), 0)
    n_past = lax.broadcasted_iota(jnp.int32, (nb, s_len), 1) // MOBA_BLOCK
    valid = blk < n_past
    for hh in range(2):
        qm = jnp.where(row_head == hh, qT, 0.0)
        for i in range(nb):
            qm_ref[hh, i] = (qm[:, i * MOBA_BLOCK:(i + 1) * MOBA_BLOCK] * SM_SCALE).astype(BF16)
        sc = jnp.dot(jnp.where(lane_head == hh, kmean, 0.0), qT, preferred_element_type=F32,
                     precision=lax.Precision.HIGHEST)
        sc = jnp.where(valid, sc, NEG_INF)
        rank = jnp.zeros((nb, s_len), F32)
        for m in range(nb):
            row = sc[m:m + 1, :]
            beats = (row > sc) | ((row == sc) & (blk > m))
            rank = rank + jnp.where(beats, 1.0, 0.0)
        sel = valid & (rank < float(MOBA_TOPK))
        for j in range(nb):
            sel_ref[hh, j] = jnp.where(sel[j:j + 1, :], 1.0, 0.0)

    key_pos = lax.broadcasted_iota(jnp.int32, (MOBA_BLOCK, MOBA_BLOCK), 0)
    q_pos = lax.broadcasted_iota(jnp.int32, (MOBA_BLOCK, MOBA_BLOCK), 1)
    causal = key_pos <= q_pos
    for i in range(nb):
        qs = slice(i * MOBA_BLOCK, (i + 1) * MOBA_BLOCK)
        outs = []
        for hh in range(2):
            rows = slice(hh * HEAD_DIM, (hh + 1) * HEAD_DIM)
            qm = qm_ref[hh, i]
            st = jnp.where(causal, _bdot(kb_ref[i], qm), NEG_INF)
            m0 = jnp.max(st, axis=0, keepdims=True)
            p = jnp.exp(st - m0)
            l0 = jnp.sum(p, axis=0, keepdims=True)
            acc0 = _bdot(vb_ref[i, rows, :], p.astype(BF16))

            def body(j, carry, hh=hh, qs=qs, rows=rows, qm=qm):
                m_run, l_run, acc = carry
                st = _bdot(kb_ref[j], qm)
                st = jnp.where(sel_ref[hh, j, :, qs] > 0.0, st, NEG_INF)
                m_new = jnp.maximum(m_run, jnp.max(st, axis=0, keepdims=True))
                alpha = jnp.exp(m_run - m_new)
                p = jnp.exp(st - m_new)
                l_new = alpha * l_run + jnp.sum(p, axis=0, keepdims=True)
                acc = alpha * acc + _bdot(vb_ref[j, rows, :], p.astype(BF16))
                return m_new, l_new, acc

            if i > 0:
                _, l_fin, acc_fin = lax.fori_loop(0, i, body, (m0, l0, acc0))
            else:
                l_fin, acc_fin = l0, acc0
            outs.append(acc_fin / l_fin)
        o_ref[0, qs, :] = jnp.concatenate(outs, axis=0).T.astype(BF16)


def _moba_prompt(qT, kT, vT):
    b, _, s = qT.shape
    hp = 2 * HEAD_DIM
    nb = s // MOBA_BLOCK
    spec = pl.BlockSpec((1, hp, s), lambda i, p: (i, p, 0))
    return pl.pallas_call(
        _moba_kernel,
        grid=(b, ATTN_WIDTH // hp),
        in_specs=[spec, spec, spec],
        out_specs=pl.BlockSpec((1, s, hp), lambda i, p: (i, 0, p)),
        out_shape=jax.ShapeDtypeStruct((b, s, ATTN_WIDTH), BF16),
        scratch_shapes=[pltpu.VMEM((nb, MOBA_BLOCK, hp), BF16),
                        pltpu.VMEM((nb, hp, MOBA_BLOCK), BF16),
                        pltpu.VMEM((2, nb, hp, MOBA_BLOCK), BF16),
                        pltpu.VMEM((2, nb, 1, s), F32)],
        compiler_params=_cparams("arbitrary", "arbitrary"),
        name="moba_prompt",
    )(qT, kT, vT)


def _post_kernel(x_ref, d_ref, gp_ref, ga_ref, at_ref, gate_ref, lin_ref, ps_ref, wbp_ref, wba_ref,
                 wout_ref, g_ref, o_ref):
    d = d_ref[...]
    mixed = jnp.concatenate(
        [_bdot(d[:, g * POOL_GROUP_WIDTH:(g + 1) * POOL_GROUP_WIDTH], lin_ref[g]) for g in range(len(POOL_WINDOWS))],
        axis=-1) * ps_ref[...]
    merged = (_sigmoid(gp_ref[...]) * _bdot(mixed.astype(BF16), wbp_ref[...])
              + _sigmoid(ga_ref[...]) * _bdot(at_ref[...], wba_ref[...]))
    mo = _bdot(merged.astype(BF16), wout_ref[...])
    o_ref[...] = x_ref[...] + gate_ref[0] * _rms(mo, g_ref[...])


def _post(x, d, gp, ga, attn, mod, lin, ps, wbp, wba, wout, g, *, tm, seq):
    n_tok = x.shape[0]
    tpg = seq // tm
    tok = lambda w: pl.BlockSpec((tm, w), lambda i: (i, 0))
    full = lambda a: pl.BlockSpec(a.shape, lambda i: (0,) * a.ndim)
    return pl.pallas_call(
        _post_kernel,
        grid=(n_tok // tm,),
        in_specs=[tok(D_MODEL), tok(POOL_WIDTH), tok(D_MODEL), tok(D_MODEL), tok(ATTN_WIDTH),
                  _mod_spec(mod, 2, tpg), full(lin), full(ps), full(wbp), full(wba), full(wout), full(g)],
        out_specs=tok(D_MODEL),
        out_shape=jax.ShapeDtypeStruct((n_tok, D_MODEL), F32),
        compiler_params=_cparams("arbitrary"),
        name="mixer_post",
    )(x, d, gp, ga, attn, mod, lin, ps, wbp, wba, wout, g)


def _ffn_kernel(x_ref, sc_ref, sh_ref, gate_ref, gpre_ref, gpost_ref, w1_ref, w2_ref, o_ref):
    x = x_ref[...]
    h = (_rms(x, gpre_ref[...]) * (1.0 + sc_ref[0]) + sh_ref[0]).astype(BF16)
    ff = jnp.zeros(x.shape, F32)
    for c in range(D_FF // D_MODEL):
        cols = slice(c * D_MODEL, (c + 1) * D_MODEL)
        a = jnp.maximum(_bdot(h, w1_ref[:, cols]), 0.0)
        ff = ff + _bdot((a * a).astype(BF16), w2_ref[cols, :])
    o_ref[...] = x + gate_ref[0] * _rms(ff, gpost_ref[...])


def _ffn(x, mod, gpre, gpost, w1, w2, *, tm, seq):
    n_tok = x.shape[0]
    tpg = seq // tm
    tok = pl.BlockSpec((tm, D_MODEL), lambda i: (i, 0))
    full = lambda a: pl.BlockSpec(a.shape, lambda i: (0,) * a.ndim)
    return pl.pallas_call(
        _ffn_kernel,
        grid=(n_tok // tm,),
        in_specs=[tok, _mod_spec(mod, 4, tpg), _mod_spec(mod, 3, tpg), _mod_spec(mod, 5, tpg),
                  full(gpre), full(gpost), full(w1), full(w2)],
        out_specs=tok,
        out_shape=jax.ShapeDtypeStruct((n_tok, D_MODEL), F32),
        compiler_params=_cparams("arbitrary"),
        name="ffn",
    )(x, mod, mod, mod, gpre, gpost, w1, w2)


def _kmean_kernel(pt_ref, ck_ref, o_ref, buf_ref, sem_ref):
    n_seq, n_pages = pt_ref.shape
    ch = KMEAN_CHUNK_PAGES
    n_chunks = n_pages // ch
    step = pl.program_id(0)

    def page_copy(layer, phys, slot, p):
        return pltpu.make_async_copy(ck_ref.at[layer, phys], buf_ref.at[slot, p], sem_ref.at[slot])

    def issue(st, c, slot):
        layer = st // n_seq
        seq = st % n_seq
        for p in range(ch):
            page_copy(layer, pt_ref[seq, c * ch + p], slot, p).start()

    @pl.when(step == 0)
    def _():
        issue(step, 0, 0)

    lane = lax.broadcasted_iota(jnp.int32, (ATTN_WIDTH, PAGE_SIZE), 1)
    acc = jnp.zeros((ATTN_WIDTH, PAGE_SIZE), F32)
    for c in range(n_chunks):
        slot = c % 2
        if c + 1 < n_chunks:
            issue(step, c + 1, 1 - slot)
        else:
            @pl.when(step + 1 < pl.num_programs(0))
            def _():
                issue(step + 1, 0, 1 - slot)
        for p in range(ch):
            page_copy(0, 0, slot, p).wait()
        for b2 in range(ch // PAGES_PER_BLOCK):
            t = buf_ref[slot, PAGES_PER_BLOCK * b2]
            for e in range(1, PAGES_PER_BLOCK):
                t = t + buf_ref[slot, PAGES_PER_BLOCK * b2 + e]
            tot = jnp.sum(t, axis=1, keepdims=True)
            acc = jnp.where(lane == c * (ch // PAGES_PER_BLOCK) + b2, tot, acc)
    o_ref[0, 0] = acc * (1.0 / MOBA_BLOCK)


def _kmean_sample(page_table, cacheT_k):
    depth = cacheT_k.shape[0]
    n_seq, n_pages = page_table.shape
    assert n_pages // PAGES_PER_BLOCK <= PAGE_SIZE and (n_pages // KMEAN_CHUNK_PAGES) % 2 == 0
    return pl.pallas_call(
        _kmean_kernel,
        grid_spec=pltpu.PrefetchScalarGridSpec(
            num_scalar_prefetch=1,
            grid=(depth * n_seq,),
            in_specs=[pl.BlockSpec(memory_space=pl.ANY)],
            out_specs=pl.BlockSpec((1, 1, ATTN_WIDTH, PAGE_SIZE), lambda s, pt: (s // n_seq, s % n_seq, 0, 0)),
            scratch_shapes=[pltpu.VMEM((2, KMEAN_CHUNK_PAGES, ATTN_WIDTH, PAGE_SIZE), F32),
                            pltpu.SemaphoreType.DMA((2,))]),
        out_shape=jax.ShapeDtypeStruct((depth, n_seq, ATTN_WIDTH, PAGE_SIZE), F32),
        compiler_params=_cparams("arbitrary"),
        name="kmean_sample",
    )(page_table, cacheT_k)


def _select_kernel(q_ref, km_ref, o_ref, *, n_blocks):
    sc = jnp.dot(q_ref[0], km_ref[0, 0], preferred_element_type=F32, precision=lax.Precision.HIGHEST)
    lane = lax.broadcasted_iota(jnp.int32, sc.shape, 1)
    cur = jnp.where(lane < n_blocks, sc, -jnp.inf)
    out = jnp.zeros(sc.shape, jnp.int32)
    for r in range(MOBA_TOPK):
        m = jnp.max(cur, axis=1, keepdims=True)
        pick = jnp.min(jnp.where(cur == m, lane, PAGE_SIZE), axis=1, keepdims=True)
        out = jnp.where(lane == r, pick, out)
        cur = jnp.where(lane == pick, -jnp.inf, cur)
    o_ref[0] = out


def _select_sample(q_exp, kmeanT, layer, n_blocks):
    n_seq, rows, _ = q_exp.shape
    return pl.pallas_call(
        functools.partial(_select_kernel, n_blocks=n_blocks),
        grid=(n_seq,),
        in_specs=[pl.BlockSpec((1, rows, ATTN_WIDTH), lambda b: (b, 0, 0)),
                  pl.BlockSpec((1, 1, ATTN_WIDTH, PAGE_SIZE), lambda b: (layer, b, 0, 0))],
        out_specs=pl.BlockSpec((1, rows, PAGE_SIZE), lambda b: (b, 0, 0)),
        out_shape=jax.ShapeDtypeStruct((n_seq, rows, PAGE_SIZE), jnp.int32),
        compiler_params=_cparams("arbitrary"),
        name="select_sample",
    )(q_exp, kmeanT)


def _attend_kernel(idx_ref, pt_ref, qT_ref, kT_ref, vT_ref, ck_ref, cv_ref, o_ref, kbuf_ref, vbuf_ref, sem_ref,
                   *, layer, n_new):
    n_seq, n_pages = pt_ref.shape
    tiles_per_tok = MOBA_TOPK * PAGES_PER_BLOCK
    n_tiles = n_new * tiles_per_tok
    h = pl.program_id(0)
    b = pl.program_id(1)
    step = h * n_seq + b
    slot = step % 2

    def tile_copies(hh, phys, sl, j):
        rows = pl.ds(hh * HEAD_DIM, HEAD_DIM)
        return (pltpu.make_async_copy(ck_ref.at[layer, phys, rows, :], kbuf_ref.at[sl, j], sem_ref.at[sl, 0]),
                pltpu.make_async_copy(cv_ref.at[layer, phys, rows, :], vbuf_ref.at[sl, j], sem_ref.at[sl, 1]))

    def issue(st, sl):
        hh = st // n_seq
        bb = st % n_seq
        base = (bb * N_HEADS + hh) * (n_new * MOBA_TOPK)
        for t in range(n_new):
            for r in range(MOBA_TOPK):
                blk = idx_ref[base + t * MOBA_TOPK + r]
                for pg in range(PAGES_PER_BLOCK):
                    logical = jnp.minimum(blk * PAGES_PER_BLOCK + pg, n_pages - 1)
                    ck, cv = tile_copies(hh, pt_ref[bb, logical], sl, (t * MOBA_TOPK + r) * PAGES_PER_BLOCK + pg)
                    ck.start()
                    cv.start()

    @pl.when(step == 0)
    def _():
        issue(step, slot)

    @pl.when(step + 1 < pl.num_programs(0) * pl.num_programs(1))
    def _():
        issue(step + 1, 1 - slot)

    for j in range(n_tiles):
        ck, cv = tile_copies(0, 0, slot, j)
        ck.wait()
        cv.wait()

    @pl.when(b == 0)
    def _():
        o_ref[...] = jnp.zeros(o_ref.shape, F32)

    lane = lax.broadcasted_iota(jnp.int32, (1, qT_ref.shape[1]), 1)
    qT = qT_ref[...]
    k_new = kT_ref[...]
    v_new = vT_ref[...]
    out = o_ref[...]
    for t in range(n_new):
        col = b * n_new + t
        qcol = jnp.sum(jnp.where(lane == col, qT, 0.0), axis=1, keepdims=True) * SM_SCALE
        scores = [jnp.sum(kbuf_ref[slot, t * tiles_per_tok + j] * qcol, axis=0, keepdims=True)
                  for j in range(tiles_per_tok)]
        own_ok = (lane >= b * n_new) & (lane <= col)
        scores.append(jnp.where(own_ok, jnp.sum(k_new * qcol, axis=0, keepdims=True), NEG_INF))
        m = scores[0]
        for s in scores[1:]:
            m = jnp.maximum(m, s)
        m = jnp.max(m, axis=1, keepdims=True)
        probs = [jnp.exp(s - m) for s in scores]
        tot = probs[0]
        for p in probs[1:]:
            tot = tot + p
        denom = jnp.sum(tot, axis=1, keepdims=True)
        pv = v_new * probs[-1]
        for j in range(tiles_per_tok):
            pv = pv + vbuf_ref[slot, t * tiles_per_tok + j] * probs[j]
        o_col = jnp.sum(pv, axis=1, keepdims=True) / denom
        out = jnp.where(lane == col, o_col, out)
    o_ref[...] = out


def _attend_sample(idx_flat, page_table, qT, kT, vT, cacheT_k, cacheT_v, layer, n_new):
    n_seq = page_table.shape[0]
    n_tok = qT.shape[1]
    n_tiles = n_new * MOBA_TOPK * PAGES_PER_BLOCK
    head = pl.BlockSpec((HEAD_DIM, n_tok), lambda h, b, idx, pt: (h, 0))
    hbm = pl.BlockSpec(memory_space=pl.ANY)
    return pl.pallas_call(
        functools.partial(_attend_kernel, layer=layer, n_new=n_new),
        grid_spec=pltpu.PrefetchScalarGridSpec(
            num_scalar_prefetch=2,
            grid=(N_HEADS, n_seq),
            in_specs=[head, head, head, hbm, hbm],
            out_specs=head,
            scratch_shapes=[pltpu.VMEM((2, n_tiles, HEAD_DIM, PAGE_SIZE), F32),
                            pltpu.VMEM((2, n_tiles, HEAD_DIM, PAGE_SIZE), F32),
                            pltpu.SemaphoreType.DMA((2, 2))]),
        out_shape=jax.ShapeDtypeStruct((ATTN_WIDTH, n_tok), F32),
        compiler_params=_cparams("arbitrary", "arbitrary"),
        name="attend_sample",
    )(idx_flat, page_table, qT, kT, vT, cacheT_k, cacheT_v)


def kernel(x_prompt, x_sample, cache_k, cache_v, state_pool, page_table, c_prompt, c_sample, w_ada, b_ada, g_mix_pre, g_mix_post, w_in, pool_lin, pool_scale, w_branch_pool, w_branch_attn, w_out, g_ffn_pre, g_ffn_post, w_ff1, w_ff2):
    depth = w_ada.shape[0]
    bp, sp, _ = x_prompt.shape
    bs, ts, _ = x_sample.shape
    n_pages = page_table.shape[1]
    n_blocks = n_pages // PAGES_PER_BLOCK
    n_phys = cache_k.shape[1]

    cacheT_k = jnp.transpose(cache_k, (0, 1, 3, 4, 2)).reshape(depth, n_phys, ATTN_WIDTH, PAGE_SIZE)
    cacheT_v = jnp.transpose(cache_v, (0, 1, 3, 4, 2)).reshape(depth, n_phys, ATTN_WIDTH, PAGE_SIZE)

    ada = _ada(jnp.concatenate([c_prompt, c_sample], axis=0), w_ada, b_ada)
    kmeanT = _kmean_sample(page_table, cacheT_k)

    head_mask = (jnp.arange(ATTN_WIDTH)[None, :] // HEAD_DIM == jnp.arange(N_HEADS)[:, None]).astype(F32)

    xp = x_prompt.reshape(bp * sp, D_MODEL)
    xs = x_sample.reshape(bs * ts, D_MODEL)
    kp_l, vp_l, hp_l, ks_l, vs_l, hs_l = [], [], [], [], [], []
    for l in range(depth):
        wu = w_in[l, :, :POOL_WIDTH].astype(BF16)
        wqkvT = w_in[l, :, POOL_WIDTH:POOL_WIDTH + 3 * ATTN_WIDTH].T.astype(BF16)
        wg = w_in[l, :, POOL_WIDTH + 3 * ATTN_WIDTH:].astype(BF16)
        lin = pool_lin[l].astype(BF16)
        ps = pool_scale[l].reshape(1, POOL_WIDTH)
        wbp = w_branch_pool[l].astype(BF16)
        wba = w_branch_attn[l].astype(BF16)
        wo = w_out[l].astype(BF16)
        w1 = w_ff1[l].astype(BF16)
        w2 = w_ff2[l].astype(BF16)
        gpre = g_mix_pre[l].reshape(1, D_MODEL)
        gpost = g_mix_post[l].reshape(1, D_MODEL)
        gfpre = g_ffn_pre[l].reshape(1, D_MODEL)
        gfpost = g_ffn_post[l].reshape(1, D_MODEL)
        mod_p = ada[l, :bp].reshape(bp, 1, ADA_CHUNKS * D_MODEL)
        mod_s = jnp.repeat(ada[l, bp:], ts, axis=0).reshape(1, bs * ts, ADA_CHUNKS * D_MODEL)

        u, qT, kT, vT, gp, ga = _inproj(xp, mod_p, gpre, wu, wqkvT, wg, tm=TOKEN_TILE, seq=sp)
        d = _pool_prompt(u.reshape(bp, sp, POOL_WIDTH), tm=POOL_TILE).reshape(bp * sp, POOL_WIDTH)
        attn = _moba_prompt(qT, kT, vT).reshape(bp * sp, ATTN_WIDTH)
        x1 = _post(xp, d, gp, ga, attn, mod_p, lin, ps, wbp, wba, wo, gpost, tm=TOKEN_TILE, seq=sp)
        xp = _ffn(x1, mod_p, gfpre, gfpost, w1, w2, tm=TOKEN_TILE, seq=sp)
        kp_l.append(kT)
        vp_l.append(vT)
        hp_l.append(u.reshape(bp, sp, POOL_WIDTH)[:, sp - POOL_HIST:, :])

        n_s = bs * ts
        u_s, qT_s, kT_s, vT_s, gp_s, ga_s = _inproj(xs, mod_s, gpre, wu, wqkvT, wg, tm=n_s, seq=n_s)
        qT_s, kT_s, vT_s = qT_s[0], kT_s[0], vT_s[0]
        z = jnp.concatenate([state_pool[l], u_s.reshape(bs, ts, POOL_WIDTH)], axis=1)
        d_s = _pool_sample(jnp.transpose(z, (1, 0, 2)), ts)
        d_s = jnp.transpose(d_s, (1, 0, 2)).reshape(n_s, POOL_WIDTH)
        q_s = qT_s.T.reshape(bs, 1, ts, ATTN_WIDTH)
        q_exp = (q_s * head_mask[None, :, None, :]).reshape(bs, N_HEADS * ts, ATTN_WIDTH)
        idx = _select_sample(q_exp, kmeanT, l, n_blocks)[:, :, :MOBA_TOPK]
        attnT_s = _attend_sample(idx.reshape(-1), page_table, qT_s, kT_s, vT_s, cacheT_k, cacheT_v, l, ts)
        attn_s = attnT_s.T.astype(BF16)
        x1_s = _post(xs, d_s, gp_s, ga_s, attn_s, mod_s, lin, ps, wbp, wba, wo, gpost, tm=n_s, seq=n_s)
        xs = _ffn(x1_s, mod_s, gfpre, gfpost, w1, w2, tm=n_s, seq=n_s)
        ks_l.append(kT_s.T.reshape(bs, ts, N_HEADS, HEAD_DIM))
        vs_l.append(vT_s.T.reshape(bs, ts, N_HEADS, HEAD_DIM))
        hs_l.append(z[:, ts:, :])

    def untranspose(parts):
        t = jnp.stack(parts).reshape(depth, bp, N_HEADS, HEAD_DIM, sp)
        return jnp.transpose(t, (0, 1, 4, 2, 3))

    return (xp.reshape(bp, sp, D_MODEL), xs.reshape(bs, ts, D_MODEL),
            untranspose(kp_l), untranspose(vp_l), jnp.stack(hp_l),
            jnp.stack(ks_l), jnp.stack(vs_l), jnp.stack(hs_l))
```

```python
import functools

import jax
import jax.numpy as jnp
from jax import lax
from jax.experimental import pallas as pl
from jax.experimental.pallas import tpu as pltpu

F32 = jnp.float32
BF16 = jnp.bfloat16

D_MODEL = 1024
N_HEADS = 8
HEAD_DIM = 64
ATTN_WIDTH = N_HEADS * HEAD_DIM
MOBA_BLOCK = 256
MOBA_TOPK = 3
PAGE_SIZE = 128
PAGES_PER_BLOCK = MOBA_BLOCK // PAGE_SIZE
POOL_WINDOWS = (2, 4, 8, 16)
POOL_WIDTH = 512
POOL_GROUP_WIDTH = 128
POOL_HIST = 15
HIST_PAD = 16
D_FF = 4 * D_MODEL
ADA_CHUNKS = 6
EPS = 1e-6
NEG_INF = -1e30
SM_SCALE = HEAD_DIM ** -0.5
LOG2_E = 1.4426950408889634
V_ROWS = HEAD_DIM + 16

V7X_VMEM_LIMIT = 56 * 1024 * 1024

TOKEN_TILE = 256
KMEAN_CHUNK_PAGES = 8
KMEAN_SLOTS = 4


def _cparams(*sem):
    return pltpu.CompilerParams(dimension_semantics=sem, vmem_limit_bytes=V7X_VMEM_LIMIT)


def _rms(x, g):
    return x * lax.rsqrt(jnp.mean(x * x, axis=-1, keepdims=True) + EPS) * g


def _sigmoid(x):
    return 1.0 / (1.0 + jnp.exp(-x))


def _bdot(a, b):
    return jnp.dot(a, b, preferred_element_type=F32)


def _ada_kernel(c_ref, w_ref, b_ref, o_ref):
    o_ref[0] = _bdot(c_ref[...].astype(BF16), w_ref[0].astype(BF16)) + b_ref[0]


def _ada(c_all, w_ada, b_ada):
    depth, _, width = w_ada.shape
    n = c_all.shape[0]
    tn = 1536
    return pl.pallas_call(
        _ada_kernel,
        grid=(depth, width // tn),
        in_specs=[pl.BlockSpec((n, D_MODEL), lambda l, j: (0, 0)),
                  pl.BlockSpec((1, D_MODEL, tn), lambda l, j: (l, 0, j)),
                  pl.BlockSpec((1, 1, tn), lambda l, j: (l, 0, j))],
        out_specs=pl.BlockSpec((1, n, tn), lambda l, j: (l, 0, j)),
        out_shape=jax.ShapeDtypeStruct((depth, n, width), F32),
        compiler_params=_cparams("arbitrary", "arbitrary"),
        name="ada",
    )(c_all, w_ada, b_ada.reshape(depth, 1, width))


def _mod_spec(mod, chunk, tiles_per_group):
    rows = mod.shape[1]
    return pl.BlockSpec((1, rows, D_MODEL), lambda i: (i // tiles_per_group, 0, chunk))


def _modulated(x_ref, g_ref, sc_ref, sh_ref):
    return (_rms(x_ref[...], g_ref[...]) * (1.0 + sc_ref[0]) + sh_ref[0]).astype(BF16)


def _inproj_kernel(x_ref, sc_ref, sh_ref, g_ref, wu_ref, wqkv_ref, u_ref, qT_ref, kT_ref, vT_ref):
    h = _modulated(x_ref, g_ref, sc_ref, sh_ref)
    u_ref[...] = _bdot(h, wu_ref[...])
    qkvT = lax.dot_general(wqkv_ref[...], h, (((1,), (1,)), ((), ())), preferred_element_type=F32)
    qT_ref[0] = qkvT[0:ATTN_WIDTH]
    kT_ref[0] = qkvT[ATTN_WIDTH:2 * ATTN_WIDTH]
    vT_ref[0] = qkvT[2 * ATTN_WIDTH:3 * ATTN_WIDTH]


def _inproj(x, mod, g, wu, wqkvT, *, tm, seq):
    n_tok = x.shape[0]
    tpg = seq // tm
    n_seq = n_tok // seq
    const = lambda i: (0, 0)
    tok = lambda w: pl.BlockSpec((tm, w), lambda i: (i, 0))
    tspec = pl.BlockSpec((1, ATTN_WIDTH, tm), lambda i: (i // tpg, 0, i % tpg))
    tshape = jax.ShapeDtypeStruct((n_seq, ATTN_WIDTH, seq), F32)
    return pl.pallas_call(
        _inproj_kernel,
        grid=(n_tok // tm,),
        in_specs=[tok(D_MODEL), _mod_spec(mod, 1, tpg), _mod_spec(mod, 0, tpg),
                  pl.BlockSpec((1, D_MODEL), const),
                  pl.BlockSpec(wu.shape, const), pl.BlockSpec(wqkvT.shape, const)],
        out_specs=[tok(POOL_WIDTH), tspec, tspec, tspec],
        out_shape=[jax.ShapeDtypeStruct((n_tok, POOL_WIDTH), F32), tshape, tshape, tshape],
        compiler_params=_cparams("arbitrary"),
        name="inproj",
    )(x, mod, mod, g, wu, wqkvT)


def _pool_tile(u, z_ref, tile_in_seq):
    tm = u.shape[0]

    @pl.when(tile_in_seq == 0)
    def _():
        z_ref[0:HIST_PAD, :] = jnp.zeros((HIST_PAD, POOL_WIDTH), F32)

    @pl.when(tile_in_seq > 0)
    def _():
        z_ref[0:HIST_PAD, :] = z_ref[tm:tm + HIST_PAD, :]

    z_ref[HIST_PAD:HIST_PAD + tm, :] = u
    pos = tile_in_seq * tm + lax.broadcasted_iota(jnp.int32, (tm, POOL_GROUP_WIDTH), 0)
    parts = []
    for g, w in enumerate(POOL_WINDOWS):
        cols = slice(g * POOL_GROUP_WIDTH, (g + 1) * POOL_GROUP_WIDTH)
        cur = u[:, cols]
        acc = cur
        for r in range(1, w):
            acc = acc + z_ref[HIST_PAD - r:HIST_PAD - r + tm, cols]
        cnt = jnp.minimum(pos + 1, w).astype(F32)
        parts.append((acc / cnt - cur).astype(BF16))
    return parts


def _pool_sample_kernel(z_ref, d_ref):
    n_new = d_ref.shape[0]
    for t in range(n_new):
        for g, w in enumerate(POOL_WINDOWS):
            cols = slice(g * POOL_GROUP_WIDTH, (g + 1) * POOL_GROUP_WIDTH)
            cur = z_ref[POOL_HIST + t, :, cols]
            acc = cur
            for r in range(1, w):
                acc = acc + z_ref[POOL_HIST + t - r, :, cols]
            d_ref[t, :, cols] = (acc / float(w) - cur).astype(BF16)


def _pool_sample(z_tm, n_new):
    _, bd, _ = z_tm.shape
    return pl.pallas_call(
        _pool_sample_kernel,
        out_shape=jax.ShapeDtypeStruct((n_new, bd, POOL_WIDTH), BF16),
        name="pool_sample",
    )(z_tm)


def _moba_kernel(qT_ref, kT_ref, vT_ref, o_ref, kb_ref, vb_ref, qm_ref, sel_ref, st_ref):
    hp = 2 * HEAD_DIM
    s_len = qT_ref.shape[2]
    nb = s_len // MOBA_BLOCK
    qT = qT_ref[0]
    k = kT_ref[0].T
    kmean = jnp.sum(k.reshape(nb, MOBA_BLOCK, hp), axis=1) * (1.0 / MOBA_BLOCK)
    kb_ref[...] = k.astype(BF16).reshape(nb, MOBA_BLOCK, hp)
    ones = jnp.ones((V_ROWS - HEAD_DIM, MOBA_BLOCK), BF16)
    for j in range(nb):
        for hh in range(2):
            vb_ref[j, hh, 0:HEAD_DIM, :] = vT_ref[0, hh * HEAD_DIM:(hh + 1) * HEAD_DIM,
                                                  j * MOBA_BLOCK:(j + 1) * MOBA_BLOCK].astype(BF16)
            vb_ref[j, hh, HEAD_DIM:V_ROWS, :] = ones

    row_head = lax.broadcasted_iota(jnp.int32, (hp, s_len), 0) // HEAD_DIM
    lane_head = lax.broadcasted_iota(jnp.int32, (nb, hp), 1) // HEAD_DIM
    blk = lax.broadcasted_iota(jnp.int32, (nb, s_len), 0)
    n_past = lax.broadcasted_iota(jnp.int32, (nb, s_len), 1) // MOBA_BLOCK
    valid = blk < n_past
    for hh in range(2):
        qm = jnp.where(row_head == hh, qT, 0.0)
        for i in range(nb):
            qm_ref[hh, i] = (qm[:, i * MOBA_BLOCK:(i + 1) * MOBA_BLOCK] * (SM_SCALE * LOG2_E)).astype(BF16)
        sc = jnp.dot(jnp.where(lane_head == hh, kmean, 0.0), qT, preferred_element_type=F32,
                     precision=lax.Precision.HIGHEST)
        sc = jnp.where(valid, sc, NEG_INF)
        rank = jnp.zeros((nb, s_len), F32)
        for m in range(nb):
            row = sc[m:m + 1, :]
            beats = (row > sc) | ((row == sc) & (blk > m))
            rank = rank + jnp.where(beats, 1.0, 0.0)
        sel = valid & (rank < float(MOBA_TOPK))
        for j in range(nb):
            sel_ref[hh, j] = jnp.where(sel[j:j + 1, :], 1.0, 0.0)

    key_pos = lax.broadcasted_iota(jnp.int32, (MOBA_BLOCK, MOBA_BLOCK), 0)
    q_pos = lax.broadcasted_iota(jnp.int32, (MOBA_BLOCK, MOBA_BLOCK), 1)
    causal = key_pos <= q_pos
    for i in range(nb):
        qs = slice(i * MOBA_BLOCK, (i + 1) * MOBA_BLOCK)
        par = i % 2
        outs = []
        for hh in range(2):
            qm = qm_ref[hh, i]
            picked = [sel_ref[hh, j, :, qs] > 0.0 for j in range(i)]
            m_run = None
            for j in range(i + 1):
                st = _bdot(kb_ref[j], qm)
                if j == i:
                    st = jnp.where(causal, st, NEG_INF)
                st_ref[par, hh, j] = st
                cm = jnp.max(st, axis=0, keepdims=True)
                if j < i:
                    cm = jnp.where(picked[j], cm, NEG_INF)
                m_run = cm if m_run is None else jnp.maximum(m_run, cm)
            acc = None
            for j in range(i + 1):
                shift = m_run if j == i else jnp.where(picked[j], m_run, -NEG_INF)
                p = jnp.exp2(st_ref[par, hh, j] - shift).astype(BF16)
                pv = _bdot(vb_ref[j, hh], p)
                acc = pv if acc is None else acc + pv
            outs.append(acc[0:HEAD_DIM] / acc[HEAD_DIM:HEAD_DIM + 1])
        o_ref[0, qs, :] = jnp.concatenate(outs, axis=0).T.astype(BF16)


def _moba_prompt(qT, kT, vT):
    b, _, s = qT.shape
    hp = 2 * HEAD_DIM
    nb = s // MOBA_BLOCK
    spec = pl.BlockSpec((1, hp, s), lambda i, p: (i, p, 0))
    return pl.pallas_call(
        _moba_kernel,
        grid=(b, ATTN_WIDTH // hp),
        in_specs=[spec, spec, spec],
        out_specs=pl.BlockSpec((1, s, hp), lambda i, p: (i, 0, p)),
        out_shape=jax.ShapeDtypeStruct((b, s, ATTN_WIDTH), BF16),
        scratch_shapes=[pltpu.VMEM((nb, MOBA_BLOCK, hp), BF16),
                        pltpu.VMEM((nb, 2, V_ROWS, MOBA_BLOCK), BF16),
                        pltpu.VMEM((2, nb, hp, MOBA_BLOCK), BF16),
                        pltpu.VMEM((2, nb, 1, s), F32),
                        pltpu.VMEM((2, 2, nb, MOBA_BLOCK, MOBA_BLOCK), F32)],
        compiler_params=_cparams("arbitrary", "arbitrary"),
        name="moba_prompt",
    )(qT, kT, vT)


def _tail_kernel(x_ref, p_ref, at_ref, sc1_ref, sh1_ref, gt1_ref, sc2_ref, sh2_ref, gt2_ref,
                 gpre_ref, gpost_ref, gfpre_ref, gfpost_ref, wg_ref, lin_ref, ps_ref, wbp_ref, wba_ref, wout_ref,
                 w1_ref, w2_ref, o_ref, *scratch, pool_in_kernel, tiles_per_seq):
    x = x_ref[...]
    h = _modulated(x_ref, gpre_ref, sc1_ref, sh1_ref)
    if pool_in_kernel:
        d_parts = _pool_tile(p_ref[...], scratch[0], pl.program_id(0) % tiles_per_seq)
    else:
        d_parts = [p_ref[:, g * POOL_GROUP_WIDTH:(g + 1) * POOL_GROUP_WIDTH] for g in range(len(POOL_WINDOWS))]
    mixed = jnp.concatenate([_bdot(d_parts[g], lin_ref[g]) for g in range(len(POOL_WINDOWS))], axis=-1) * ps_ref[...]
    merged = (_sigmoid(_bdot(h, wg_ref[:, 0:D_MODEL])) * _bdot(mixed.astype(BF16), wbp_ref[...])
              + _sigmoid(_bdot(h, wg_ref[:, D_MODEL:2 * D_MODEL])) * _bdot(at_ref[...], wba_ref[...]))
    x1 = x + gt1_ref[0] * _rms(_bdot(merged.astype(BF16), wout_ref[...]), gpost_ref[...])
    h2 = (_rms(x1, gfpre_ref[...]) * (1.0 + sc2_ref[0]) + sh2_ref[0]).astype(BF16)
    ff = jnp.zeros(x.shape, F32)
    for c in range(D_FF // D_MODEL):
        cols = slice(c * D_MODEL, (c + 1) * D_MODEL)
        a = jnp.maximum(_bdot(h2, w1_ref[:, cols]), 0.0)
        ff = ff + _bdot((a * a).astype(BF16), w2_ref[cols, :])
    o_ref[...] = x1 + gt2_ref[0] * _rms(ff, gfpost_ref[...])


def _tail(x, pool_in, attn, mod, gains, weights, *, tm, seq, pool_in_kernel):
    n_tok = x.shape[0]
    tpg = seq // tm
    tok = lambda w: pl.BlockSpec((tm, w), lambda i: (i, 0))
    full = lambda a: pl.BlockSpec(a.shape, lambda i: (0,) * a.ndim, pipeline_mode=pl.Buffered(1))
    mods = [_mod_spec(mod, c, tpg) for c in (1, 0, 2, 4, 3, 5)]
    return pl.pallas_call(
        functools.partial(_tail_kernel, pool_in_kernel=pool_in_kernel, tiles_per_seq=tpg),
        grid=(n_tok // tm,),
        in_specs=[tok(D_MODEL), tok(POOL_WIDTH), tok(ATTN_WIDTH)] + mods
                 + [full(a) for a in gains] + [full(a) for a in weights],
        out_specs=tok(D_MODEL),
        out_shape=jax.ShapeDtypeStruct((n_tok, D_MODEL), F32),
        scratch_shapes=[pltpu.VMEM((HIST_PAD + tm, POOL_WIDTH), F32)] if pool_in_kernel else [],
        compiler_params=_cparams("arbitrary"),
        name="tail",
    )(x, pool_in, attn, *([mod] * 6), *gains, *weights)


def _kmean_kernel(pt_ref, ck_ref, o_ref, buf_ref, sem_ref):
    n_seq, n_pages = pt_ref.shape
    ch = KMEAN_CHUNK_PAGES
    n_chunks = n_pages // ch
    ahead = KMEAN_SLOTS - 1
    step = pl.program_id(0)

    def page_copy(layer, phys, slot, p):
        return pltpu.make_async_copy(ck_ref.at[layer, phys], buf_ref.at[slot, p], sem_ref.at[slot])

    def issue(st, c):
        layer = st // n_seq
        seq = st % n_seq
        for p in range(ch):
            page_copy(layer, pt_ref[seq, c * ch + p], c % KMEAN_SLOTS, p).start()

    @pl.when(step == 0)
    def _():
        for c in range(ahead):
            issue(step, c)

    lane = lax.broadcasted_iota(jnp.int32, (ATTN_WIDTH, PAGE_SIZE), 1)
    acc = jnp.zeros((ATTN_WIDTH, PAGE_SIZE), F32)
    for c in range(n_chunks):
        slot = c % KMEAN_SLOTS
        nxt = c + ahead
        if nxt < n_chunks:
            issue(step, nxt)
        else:
            @pl.when(step + 1 < pl.num_programs(0))
            def _():
                issue(step + 1, nxt - n_chunks)
        for p in range(ch):
            page_copy(0, 0, slot, p).wait()
        for b2 in range(ch // PAGES_PER_BLOCK):
            t = buf_ref[slot, PAGES_PER_BLOCK * b2]
            for e in range(1, PAGES_PER_BLOCK):
                t = t + buf_ref[slot, PAGES_PER_BLOCK * b2 + e]
            tot = jnp.sum(t, axis=1, keepdims=True)
            acc = jnp.where(lane == c * (ch // PAGES_PER_BLOCK) + b2, tot, acc)
    o_ref[0, 0] = acc * (1.0 / MOBA_BLOCK)


def _kmean_sample(page_table, cacheT_k):
    depth = cacheT_k.shape[0]
    n_seq, n_pages = page_table.shape
    assert n_pages // PAGES_PER_BLOCK <= PAGE_SIZE and (n_pages // KMEAN_CHUNK_PAGES) % KMEAN_SLOTS == 0
    return pl.pallas_call(
        _kmean_kernel,
        grid_spec=pltpu.PrefetchScalarGridSpec(
            num_scalar_prefetch=1,
            grid=(depth * n_seq,),
            in_specs=[pl.BlockSpec(memory_space=pl.ANY)],
            out_specs=pl.BlockSpec((1, 1, ATTN_WIDTH, PAGE_SIZE), lambda s, pt: (s // n_seq, s % n_seq, 0, 0)),
            scratch_shapes=[pltpu.VMEM((KMEAN_SLOTS, KMEAN_CHUNK_PAGES, ATTN_WIDTH, PAGE_SIZE), F32),
                            pltpu.SemaphoreType.DMA((KMEAN_SLOTS,))]),
        out_shape=jax.ShapeDtypeStruct((depth, n_seq, ATTN_WIDTH, PAGE_SIZE), F32),
        compiler_params=_cparams("arbitrary"),
        name="kmean_sample",
    )(page_table, cacheT_k)


def _select_kernel(q_ref, km_ref, o_ref, *, n_blocks):
    sc = jnp.dot(q_ref[0], km_ref[0, 0], preferred_element_type=F32, precision=lax.Precision.HIGHEST)
    lane = lax.broadcasted_iota(jnp.int32, sc.shape, 1)
    cur = jnp.where(lane < n_blocks, sc, -jnp.inf)
    out = jnp.zeros(sc.shape, jnp.int32)
    for r in range(MOBA_TOPK):
        m = jnp.max(cur, axis=1, keepdims=True)
        pick = jnp.min(jnp.where(cur == m, lane, PAGE_SIZE), axis=1, keepdims=True)
        out = jnp.where(lane == r, pick, out)
        cur = jnp.where(lane == pick, -jnp.inf, cur)
    o_ref[0] = out


def _select_sample(q_exp, kmeanT, layer, n_blocks):
    n_seq, rows, _ = q_exp.shape
    return pl.pallas_call(
        functools.partial(_select_kernel, n_blocks=n_blocks),
        grid=(n_seq,),
        in_specs=[pl.BlockSpec((1, rows, ATTN_WIDTH), lambda b: (b, 0, 0)),
                  pl.BlockSpec((1, 1, ATTN_WIDTH, PAGE_SIZE), lambda b: (layer, b, 0, 0))],
        out_specs=pl.BlockSpec((1, rows, PAGE_SIZE), lambda b: (b, 0, 0)),
        out_shape=jax.ShapeDtypeStruct((n_seq, rows, PAGE_SIZE), jnp.int32),
        compiler_params=_cparams("arbitrary"),
        name="select_sample",
    )(q_exp, kmeanT)


def _attend_kernel(idx_ref, pt_ref, qT_ref, kT_ref, vT_ref, ck_ref, cv_ref, o_ref, kbuf_ref, vbuf_ref, sem_ref,
                   *, layer, n_new):
    n_seq, n_pages = pt_ref.shape
    tiles_per_tok = MOBA_TOPK * PAGES_PER_BLOCK
    n_tiles = n_new * tiles_per_tok
    h = pl.program_id(0)
    b = pl.program_id(1)
    step = h * n_seq + b
    slot = step % 2

    def tile_copies(hh, phys, sl, j):
        rows = pl.ds(hh * HEAD_DIM, HEAD_DIM)
        return (pltpu.make_async_copy(ck_ref.at[layer, phys, rows, :], kbuf_ref.at[sl, j], sem_ref.at[sl, 0]),
                pltpu.make_async_copy(cv_ref.at[layer, phys, rows, :], vbuf_ref.at[sl, j], sem_ref.at[sl, 1]))

    def issue(st, sl):
        hh = st // n_seq
        bb = st % n_seq
        base = (bb * N_HEADS + hh) * (n_new * MOBA_TOPK)
        for t in range(n_new):
            for r in range(MOBA_TOPK):
                blk = idx_ref[base + t * MOBA_TOPK + r]
                for pg in range(PAGES_PER_BLOCK):
                    logical = jnp.minimum(blk * PAGES_PER_BLOCK + pg, n_pages - 1)
                    ck, cv = tile_copies(hh, pt_ref[bb, logical], sl, (t * MOBA_TOPK + r) * PAGES_PER_BLOCK + pg)
                    ck.start()
                    cv.start()

    @pl.when(step == 0)
    def _():
        issue(step, slot)

    @pl.when(step + 1 < pl.num_programs(0) * pl.num_programs(1))
    def _():
        issue(step + 1, 1 - slot)

    for j in range(n_tiles):
        ck, cv = tile_copies(0, 0, slot, j)
        ck.wait()
        cv.wait()

    @pl.when(b == 0)
    def _():
        o_ref[...] = jnp.zeros(o_ref.shape, F32)

    lane = lax.broadcasted_iota(jnp.int32, (1, qT_ref.shape[1]), 1)
    qT = qT_ref[...]
    k_new = kT_ref[...]
    v_new = vT_ref[...]
    out = o_ref[...]
    for t in range(n_new):
        col = b * n_new + t
        qcol = jnp.sum(jnp.where(lane == col, qT, 0.0), axis=1, keepdims=True) * SM_SCALE
        scores = [jnp.sum(kbuf_ref[slot, t * tiles_per_tok + j] * qcol, axis=0, keepdims=True)
                  for j in range(tiles_per_tok)]
        own_ok = (lane >= b * n_new) & (lane <= col)
        scores.append(jnp.where(own_ok, jnp.sum(k_new * qcol, axis=0, keepdims=True), NEG_INF))
        m = scores[0]
        for s in scores[1:]:
            m = jnp.maximum(m, s)
        m = jnp.max(m, axis=1, keepdims=True)
        probs = [jnp.exp(s - m) for s in scores]
        tot = probs[0]
        for p in probs[1:]:
            tot = tot + p
        denom = jnp.sum(tot, axis=1, keepdims=True)
        pv = v_new * probs[-1]
        for j in range(tiles_per_tok):
            pv = pv + vbuf_ref[slot, t * tiles_per_tok + j] * probs[j]
        o_col = jnp.sum(pv, axis=1, keepdims=True) / denom
        out = jnp.where(lane == col, o_col, out)
    o_ref[...] = out


def _attend_sample(idx_flat, page_table, qT, kT, vT, cacheT_k, cacheT_v, layer, n_new):
    n_seq = page_table.shape[0]
    n_tok = qT.shape[1]
    n_tiles = n_new * MOBA_TOPK * PAGES_PER_BLOCK
    head = pl.BlockSpec((HEAD_DIM, n_tok), lambda h, b, idx, pt: (h, 0))
    hbm = pl.BlockSpec(memory_space=pl.ANY)
    return pl.pallas_call(
        functools.partial(_attend_kernel, layer=layer, n_new=n_new),
        grid_spec=pltpu.PrefetchScalarGridSpec(
            num_scalar_prefetch=2,
            grid=(N_HEADS, n_seq),
            in_specs=[head, head, head, hbm, hbm],
            out_specs=head,
            scratch_shapes=[pltpu.VMEM((2, n_tiles, HEAD_DIM, PAGE_SIZE), F32),
                            pltpu.VMEM((2, n_tiles, HEAD_DIM, PAGE_SIZE), F32),
                            pltpu.SemaphoreType.DMA((2, 2))]),
        out_shape=jax.ShapeDtypeStruct((ATTN_WIDTH, n_tok), F32),
        compiler_params=_cparams("arbitrary", "arbitrary"),
        name="attend_sample",
    )(idx_flat, page_table, qT, kT, vT, cacheT_k, cacheT_v)


def kernel(x_prompt, x_sample, cache_k, cache_v, state_pool, page_table, c_prompt, c_sample, w_ada, b_ada, g_mix_pre, g_mix_post, w_in, pool_lin, pool_scale, w_branch_pool, w_branch_attn, w_out, g_ffn_pre, g_ffn_post, w_ff1, w_ff2):
    depth = w_ada.shape[0]
    bp, sp, _ = x_prompt.shape
    bs, ts, _ = x_sample.shape
    n_pages = page_table.shape[1]
    n_blocks = n_pages // PAGES_PER_BLOCK
    n_phys = cache_k.shape[1]

    cacheT_k = jnp.transpose(cache_k, (0, 1, 3, 4, 2)).reshape(depth, n_phys, ATTN_WIDTH, PAGE_SIZE)
    cacheT_v = jnp.transpose(cache_v, (0, 1, 3, 4, 2)).reshape(depth, n_phys, ATTN_WIDTH, PAGE_SIZE)

    ada = _ada(jnp.concatenate([c_prompt, c_sample], axis=0), w_ada, b_ada)
    kmeanT = _kmean_sample(page_table, cacheT_k)

    head_mask = (jnp.arange(ATTN_WIDTH)[None, :] // HEAD_DIM == jnp.arange(N_HEADS)[:, None]).astype(F32)

    xp = x_prompt.reshape(bp * sp, D_MODEL)
    xs = x_sample.reshape(bs * ts, D_MODEL)
    kp_l, vp_l, hp_l, ks_l, vs_l, hs_l = [], [], [], [], [], []
    for l in range(depth):
        wu = w_in[l, :, :POOL_WIDTH].astype(BF16)
        wqkvT = w_in[l, :, POOL_WIDTH:POOL_WIDTH + 3 * ATTN_WIDTH].T.astype(BF16)
        wg = w_in[l, :, POOL_WIDTH + 3 * ATTN_WIDTH:].astype(BF16)
        lin = pool_lin[l].astype(BF16)
        ps = pool_scale[l].reshape(1, POOL_WIDTH)
        wbp = w_branch_pool[l].astype(BF16)
        wba = w_branch_attn[l].astype(BF16)
        wo = w_out[l].astype(BF16)
        w1 = w_ff1[l].astype(BF16)
        w2 = w_ff2[l].astype(BF16)
        gpre = g_mix_pre[l].reshape(1, D_MODEL)
        gpost = g_mix_post[l].reshape(1, D_MODEL)
        gfpre = g_ffn_pre[l].reshape(1, D_MODEL)
        gfpost = g_ffn_post[l].reshape(1, D_MODEL)
        gains = (gpre, gpost, gfpre, gfpost)
        weights = (wg, lin, ps, wbp, wba, wo, w1, w2)
        mod_p = ada[l, :bp].reshape(bp, 1, ADA_CHUNKS * D_MODEL)
        mod_s = jnp.repeat(ada[l, bp:], ts, axis=0).reshape(1, bs * ts, ADA_CHUNKS * D_MODEL)

        u, qT, kT, vT = _inproj(xp, mod_p, gpre, wu, wqkvT, tm=TOKEN_TILE, seq=sp)
        attn = _moba_prompt(qT, kT, vT).reshape(bp * sp, ATTN_WIDTH)
        xp = _tail(xp, u, attn, mod_p, gains, weights, tm=TOKEN_TILE, seq=sp, pool_in_kernel=True)
        kp_l.append(kT)
        vp_l.append(vT)
        hp_l.append(u.reshape(bp, sp, POOL_WIDTH)[:, sp - POOL_HIST:, :])

        n_s = bs * ts
        u_s, qT_s, kT_s, vT_s = _inproj(xs, mod_s, gpre, wu, wqkvT, tm=n_s, seq=n_s)
        qT_s, kT_s, vT_s = qT_s[0], kT_s[0], vT_s[0]
        z = jnp.concatenate([state_pool[l], u_s.reshape(bs, ts, POOL_WIDTH)], axis=1)
        d_s = _pool_sample(jnp.transpose(z, (1, 0, 2)), ts)
        d_s = jnp.transpose(d_s, (1, 0, 2)).reshape(n_s, POOL_WIDTH)
        q_s = qT_s.T.reshape(bs, 1, ts, ATTN_WIDTH)
        q_exp = (q_s * head_mask[None, :, None, :]).reshape(bs, N_HEADS * ts, ATTN_WIDTH)
        idx = _select_sample(q_exp, kmeanT, l, n_blocks)[:, :, :MOBA_TOPK]
        attnT_s = _attend_sample(idx.reshape(-1), page_table, qT_s, kT_s, vT_s, cacheT_k, cacheT_v, l, ts)
        attn_s = attnT_s.T.astype(BF16)
        xs = _tail(xs, d_s, attn_s, mod_s, gains, weights, tm=n_s, seq=n_s, pool_in_kernel=False)
        ks_l.append(kT_s.T.reshape(bs, ts, N_HEADS, HEAD_DIM))
        vs_l.append(vT_s.T.reshape(bs, ts, N_HEADS, HEAD_DIM))
        hs_l.append(z[:, ts:, :])

    def untranspose(parts):
        t = jnp.stack(parts).reshape(depth, bp, N_HEADS, HEAD_DIM, sp)
        return jnp.transpose(t, (0, 1, 4, 2, 3))

    return (xp.reshape(bp, sp, D_MODEL), xs.reshape(bs, ts, D_MODEL),
            untranspose(kp_l), untranspose(vp_l), jnp.stack(hp_l),
            jnp.stack(ks_l), jnp.stack(vs_l), jnp.stack(hs_l))
```

```python
import functools

import jax
import jax.numpy as jnp
from jax import lax
from jax.experimental import pallas as pl
from jax.experimental.pallas import tpu as pltpu

F32 = jnp.float32
BF16 = jnp.bfloat16

D_MODEL = 1024
N_HEADS = 8
HEAD_DIM = 64
ATTN_WIDTH = N_HEADS * HEAD_DIM
MOBA_BLOCK = 256
MOBA_TOPK = 3
PAGE_SIZE = 128
PAGES_PER_BLOCK = MOBA_BLOCK // PAGE_SIZE
POOL_WINDOWS = (2, 4, 8, 16)
POOL_WIDTH = 512
POOL_GROUP_WIDTH = 128
POOL_HIST = 15
HIST_PAD = 16
D_FF = 4 * D_MODEL
ADA_CHUNKS = 6
EPS = 1e-6
NEG_INF = -1e30
SM_SCALE = HEAD_DIM ** -0.5
LOG2_E = 1.4426950408889634
V_ROWS = HEAD_DIM + 16

V7X_VMEM_LIMIT = 56 * 1024 * 1024

TOKEN_TILE = 256
STREAM_CHUNKS = 4


def _cparams(*sem):
    return pltpu.CompilerParams(dimension_semantics=sem, vmem_limit_bytes=V7X_VMEM_LIMIT)


def _rms(x, g):
    return x * lax.rsqrt(jnp.mean(x * x, axis=-1, keepdims=True) + EPS) * g


def _sigmoid(x):
    return 1.0 / (1.0 + jnp.exp(-x))


def _bdot(a, b):
    return jnp.dot(a, b, preferred_element_type=F32)


def _ada_kernel(c_ref, w_ref, b_ref, o_ref):
    o_ref[0] = _bdot(c_ref[...].astype(BF16), w_ref[0].astype(BF16)) + b_ref[0]


def _ada(c_all, w_ada, b_ada):
    depth, _, width = w_ada.shape
    n = c_all.shape[0]
    tn = 1536
    return pl.pallas_call(
        _ada_kernel,
        grid=(depth, width // tn),
        in_specs=[pl.BlockSpec((n, D_MODEL), lambda l, j: (0, 0)),
                  pl.BlockSpec((1, D_MODEL, tn), lambda l, j: (l, 0, j)),
                  pl.BlockSpec((1, 1, tn), lambda l, j: (l, 0, j))],
        out_specs=pl.BlockSpec((1, n, tn), lambda l, j: (l, 0, j)),
        out_shape=jax.ShapeDtypeStruct((depth, n, width), F32),
        compiler_params=_cparams("arbitrary", "arbitrary"),
        name="ada",
    )(c_all, w_ada, b_ada.reshape(depth, 1, width))


def _mod_spec(mod, chunk, tiles_per_group):
    rows = mod.shape[1]
    return pl.BlockSpec((1, rows, D_MODEL), lambda i: (i // tiles_per_group, 0, chunk))


def _modulated(x_ref, g_ref, sc_ref, sh_ref):
    return (_rms(x_ref[...], g_ref[...]) * (1.0 + sc_ref[0]) + sh_ref[0]).astype(BF16)


def _inproj_kernel(x_ref, sc_ref, sh_ref, g_ref, wu_ref, wqkv_ref, u_ref, qT_ref, kT_ref, vT_ref):
    h = _modulated(x_ref, g_ref, sc_ref, sh_ref)
    u_ref[...] = _bdot(h, wu_ref[...])
    qkvT = lax.dot_general(wqkv_ref[...], h, (((1,), (1,)), ((), ())), preferred_element_type=F32)
    qT_ref[0] = qkvT[0:ATTN_WIDTH]
    kT_ref[0] = qkvT[ATTN_WIDTH:2 * ATTN_WIDTH]
    vT_ref[0] = qkvT[2 * ATTN_WIDTH:3 * ATTN_WIDTH]


def _inproj(x, mod, g, wu, wqkvT, *, tm, seq):
    n_tok = x.shape[0]
    tpg = seq // tm
    n_seq = n_tok // seq
    const = lambda i: (0, 0)
    tok = lambda w: pl.BlockSpec((tm, w), lambda i: (i, 0))
    tspec = pl.BlockSpec((1, ATTN_WIDTH, tm), lambda i: (i // tpg, 0, i % tpg))
    tshape = jax.ShapeDtypeStruct((n_seq, ATTN_WIDTH, seq), F32)
    return pl.pallas_call(
        _inproj_kernel,
        grid=(n_tok // tm,),
        in_specs=[tok(D_MODEL), _mod_spec(mod, 1, tpg), _mod_spec(mod, 0, tpg),
                  pl.BlockSpec((1, D_MODEL), const),
                  pl.BlockSpec(wu.shape, const), pl.BlockSpec(wqkvT.shape, const)],
        out_specs=[tok(POOL_WIDTH), tspec, tspec, tspec],
        out_shape=[jax.ShapeDtypeStruct((n_tok, POOL_WIDTH), F32), tshape, tshape, tshape],
        compiler_params=_cparams("arbitrary"),
        name="inproj",
    )(x, mod, mod, g, wu, wqkvT)


def _pool_tile(u, z_ref, tile_in_seq):
    tm = u.shape[0]

    @pl.when(tile_in_seq == 0)
    def _():
        z_ref[0:HIST_PAD, :] = jnp.zeros((HIST_PAD, POOL_WIDTH), F32)

    @pl.when(tile_in_seq > 0)
    def _():
        z_ref[0:HIST_PAD, :] = z_ref[tm:tm + HIST_PAD, :]

    z_ref[HIST_PAD:HIST_PAD + tm, :] = u
    pos = tile_in_seq * tm + lax.broadcasted_iota(jnp.int32, (tm, POOL_GROUP_WIDTH), 0)
    parts = []
    for g, w in enumerate(POOL_WINDOWS):
        cols = slice(g * POOL_GROUP_WIDTH, (g + 1) * POOL_GROUP_WIDTH)
        cur = u[:, cols]
        acc = cur
        for r in range(1, w):
            acc = acc + z_ref[HIST_PAD - r:HIST_PAD - r + tm, cols]
        cnt = jnp.minimum(pos + 1, w).astype(F32)
        parts.append((acc / cnt - cur).astype(BF16))
    return parts


def _pool_sample_kernel(z_ref, d_ref):
    n_new = d_ref.shape[0]
    for t in range(n_new):
        for g, w in enumerate(POOL_WINDOWS):
            cols = slice(g * POOL_GROUP_WIDTH, (g + 1) * POOL_GROUP_WIDTH)
            cur = z_ref[POOL_HIST + t, :, cols]
            acc = cur
            for r in range(1, w):
                acc = acc + z_ref[POOL_HIST + t - r, :, cols]
            d_ref[t, :, cols] = (acc / float(w) - cur).astype(BF16)


def _pool_sample(z_tm, n_new):
    _, bd, _ = z_tm.shape
    return pl.pallas_call(
        _pool_sample_kernel,
        out_shape=jax.ShapeDtypeStruct((n_new, bd, POOL_WIDTH), BF16),
        name="pool_sample",
    )(z_tm)


def _moba_kernel(qT_ref, kT_ref, vT_ref, o_ref, kb_ref, vb_ref, qm_ref, sel_ref, st_ref, p_ref):
    hp = 2 * HEAD_DIM
    s_len = qT_ref.shape[2]
    nb = s_len // MOBA_BLOCK
    qT = qT_ref[0]
    k = kT_ref[0].T
    kmean = jnp.sum(k.reshape(nb, MOBA_BLOCK, hp), axis=1) * (1.0 / MOBA_BLOCK)
    kb_ref[...] = k.astype(BF16)
    ones = jnp.ones((V_ROWS - HEAD_DIM, s_len), BF16)
    for hh in range(2):
        vb_ref[hh, 0:HEAD_DIM, :] = vT_ref[0, hh * HEAD_DIM:(hh + 1) * HEAD_DIM, :].astype(BF16)
        vb_ref[hh, HEAD_DIM:V_ROWS, :] = ones

    row_head = lax.broadcasted_iota(jnp.int32, (hp, s_len), 0) // HEAD_DIM
    lane_head = lax.broadcasted_iota(jnp.int32, (nb, hp), 1) // HEAD_DIM
    blk = lax.broadcasted_iota(jnp.int32, (nb, s_len), 0)
    n_past = lax.broadcasted_iota(jnp.int32, (nb, s_len), 1) // MOBA_BLOCK
    valid = blk < n_past
    for hh in range(2):
        qm = jnp.where(row_head == hh, qT, 0.0)
        for i in range(nb):
            qm_ref[hh, i] = (qm[:, i * MOBA_BLOCK:(i + 1) * MOBA_BLOCK] * (SM_SCALE * LOG2_E)).astype(BF16)
        sc = jnp.dot(jnp.where(lane_head == hh, kmean, 0.0), qT, preferred_element_type=F32,
                     precision=lax.Precision.HIGHEST)
        sc = jnp.where(valid, sc, NEG_INF)
        rank = jnp.zeros((nb, s_len), F32)
        for m in range(nb):
            row = sc[m:m + 1, :]
            beats = (row > sc) | ((row == sc) & (blk > m))
            rank = rank + jnp.where(beats, 1.0, 0.0)
        sel_ref[hh] = jnp.where(valid & (rank < float(MOBA_TOPK)), 1.0, 0.0)

    key_pos = lax.broadcasted_iota(jnp.int32, (MOBA_BLOCK, MOBA_BLOCK), 0)
    q_pos = lax.broadcasted_iota(jnp.int32, (MOBA_BLOCK, MOBA_BLOCK), 1)
    causal = key_pos <= q_pos
    units = [(i, hh) for i in range(nb) for hh in range(2)]
    state = {}

    def blocks(j):
        return slice(j * MOBA_BLOCK, (j + 1) * MOBA_BLOCK)

    def scores(n):
        i, hh = units[n]
        picked = [sel_ref[hh, j:j + 1, blocks(i)] > 0.0 for j in range(i)]
        st_ref[n % 2, 0:(i + 1) * MOBA_BLOCK, :] = _bdot(kb_ref[0:(i + 1) * MOBA_BLOCK, :], qm_ref[hh, i])
        m_run = jnp.max(jnp.where(causal, st_ref[n % 2, blocks(i), :], NEG_INF), axis=0, keepdims=True)
        for j in range(i):
            cm = jnp.max(st_ref[n % 2, blocks(j), :], axis=0, keepdims=True)
            m_run = jnp.maximum(m_run, jnp.where(picked[j], cm, NEG_INF))
        state[n] = (m_run, picked)

    def values(n):
        i, hh = units[n]
        m_run, picked = state.pop(n)
        for j in range(i + 1):
            st = st_ref[n % 2, blocks(j), :]
            if j == i:
                st = jnp.where(causal, st, NEG_INF)
                shift = m_run
            else:
                shift = jnp.where(picked[j], m_run, -NEG_INF)
            p_ref[n % 2, blocks(j), :] = jnp.exp2(st - shift).astype(BF16)
        acc = _bdot(vb_ref[hh, :, 0:(i + 1) * MOBA_BLOCK], p_ref[n % 2, 0:(i + 1) * MOBA_BLOCK, :])
        state[("out", i, hh)] = acc[0:HEAD_DIM] / acc[HEAD_DIM:HEAD_DIM + 1]
        if hh == 1:
            both = jnp.concatenate([state.pop(("out", i, 0)), state.pop(("out", i, 1))], axis=0)
            o_ref[0, blocks(i), :] = both.T.astype(BF16)

    for n in range(len(units) + 1):
        if n < len(units):
            scores(n)
        if n > 0:
            values(n - 1)


def _moba_prompt(qT, kT, vT):
    b, _, s = qT.shape
    hp = 2 * HEAD_DIM
    nb = s // MOBA_BLOCK
    spec = pl.BlockSpec((1, hp, s), lambda i, p: (i, p, 0))
    return pl.pallas_call(
        _moba_kernel,
        grid=(b, ATTN_WIDTH // hp),
        in_specs=[spec, spec, spec],
        out_specs=pl.BlockSpec((1, s, hp), lambda i, p: (i, 0, p)),
        out_shape=jax.ShapeDtypeStruct((b, s, ATTN_WIDTH), BF16),
        scratch_shapes=[pltpu.VMEM((s, hp), BF16),
                        pltpu.VMEM((2, V_ROWS, s), BF16),
                        pltpu.VMEM((2, nb, hp, MOBA_BLOCK), BF16),
                        pltpu.VMEM((2, nb, s), F32),
                        pltpu.VMEM((2, s, MOBA_BLOCK), F32),
                        pltpu.VMEM((2, s, MOBA_BLOCK), BF16)],
        compiler_params=_cparams("arbitrary", "arbitrary"),
        name="moba_prompt",
    )(qT, kT, vT)


def _cache_stream(pt_ref, ck_ref, buf_ref, sem_ref, *, n_seq, n_pages, pages_per_step):
    cp = pages_per_step // STREAM_CHUNKS

    def page_copy(layer, phys, k, p):
        return pltpu.make_async_copy(ck_ref.at[layer, phys], buf_ref.at[k, p], sem_ref.at[k])

    def issue(st, k):
        first = st * pages_per_step + k * cp
        layer = first // (n_seq * n_pages)
        seq = (first // n_pages) % n_seq
        page0 = first % n_pages
        for p in range(cp):
            page_copy(layer, pt_ref[seq, page0 + p], k, p).start()

    def point(step, k, acc):
        for p in range(cp):
            page_copy(0, 0, k, p).wait()
        if k == 0:
            issue(step, STREAM_CHUNKS - 1)
        else:
            @pl.when(step + 1 < pl.num_programs(0))
            def _():
                issue(step + 1, k - 1)
        lane = lax.broadcasted_iota(jnp.int32, (ATTN_WIDTH, PAGE_SIZE), 1)
        block0 = ((step * pages_per_step + k * cp) % n_pages) // PAGES_PER_BLOCK
        for b2 in range(cp // PAGES_PER_BLOCK):
            t = buf_ref[k, PAGES_PER_BLOCK * b2]
            for e in range(1, PAGES_PER_BLOCK):
                t = t + buf_ref[k, PAGES_PER_BLOCK * b2 + e]
            acc = jnp.where(lane == block0 + b2, jnp.sum(t, axis=1, keepdims=True), acc)
        return acc

    return issue, point


def _tail_kernel(*refs, pool_in_kernel, tiles_per_seq, stream):
    (x_ref, p_ref, at_ref, sc1_ref, sh1_ref, gt1_ref, sc2_ref, sh2_ref, gt2_ref,
     gpre_ref, gpost_ref, gfpre_ref, gfpost_ref, wg_ref, lin_ref, ps_ref, wbp_ref, wba_ref, wout_ref,
     w1_ref, w2_ref) = refs[:21]
    rest = list(refs[21:])
    if stream is not None:
        pt_ref, ck_ref = rest[:2]
        rest = rest[2:]
    o_ref = rest.pop(0)
    if stream is not None:
        km_ref = rest.pop(0)
    z_ref = rest.pop(0) if pool_in_kernel else None
    step = pl.program_id(0)

    def stream_point(k):
        pass

    if stream is not None:
        buf_ref, sem_ref, kacc_ref = rest
        issue, point = _cache_stream(pt_ref, ck_ref, buf_ref, sem_ref, **stream)

        @pl.when(step == 0)
        def _():
            for k in range(STREAM_CHUNKS - 1):
                issue(step, k)

        @pl.when((step * stream["pages_per_step"]) % stream["n_pages"] == 0)
        def _():
            kacc_ref[...] = jnp.zeros(kacc_ref.shape, F32)

        def stream_point(k):
            acc = point(step, k, kacc_ref[...])
            kacc_ref[...] = acc
            if k == STREAM_CHUNKS - 1:
                km_ref[0, 0] = acc * (1.0 / MOBA_BLOCK)

    stream_point(0)
    x = x_ref[...]
    h = _modulated(x_ref, gpre_ref, sc1_ref, sh1_ref)
    if pool_in_kernel:
        d_parts = _pool_tile(p_ref[...], z_ref, step % tiles_per_seq)
    else:
        d_parts = [p_ref[:, g * POOL_GROUP_WIDTH:(g + 1) * POOL_GROUP_WIDTH] for g in range(len(POOL_WINDOWS))]
    mixed = jnp.concatenate([_bdot(d_parts[g], lin_ref[g]) for g in range(len(POOL_WINDOWS))], axis=-1) * ps_ref[...]
    merged = (_sigmoid(_bdot(h, wg_ref[:, 0:D_MODEL])) * _bdot(mixed.astype(BF16), wbp_ref[...])
              + _sigmoid(_bdot(h, wg_ref[:, D_MODEL:2 * D_MODEL])) * _bdot(at_ref[...], wba_ref[...]))
    stream_point(1)
    x1 = x + gt1_ref[0] * _rms(_bdot(merged.astype(BF16), wout_ref[...]), gpost_ref[...])
    h2 = (_rms(x1, gfpre_ref[...]) * (1.0 + sc2_ref[0]) + sh2_ref[0]).astype(BF16)
    stream_point(2)
    ff = jnp.zeros(x.shape, F32)
    for c in range(D_FF // D_MODEL):
        cols = slice(c * D_MODEL, (c + 1) * D_MODEL)
        a = jnp.maximum(_bdot(h2, w1_ref[:, cols]), 0.0)
        ff = ff + _bdot((a * a).astype(BF16), w2_ref[cols, :])
        if c == 1:
            stream_point(3)
    o_ref[...] = x1 + gt2_ref[0] * _rms(ff, gfpost_ref[...])


def _tail(x, pool_in, attn, mod, gains, weights, *, tm, seq, pool_in_kernel, cache=None):
    n_tok = x.shape[0]
    n_steps = n_tok // tm
    tpg = seq // tm
    tok = lambda w: pl.BlockSpec((tm, w), lambda i: (i, 0))
    full = lambda a: pl.BlockSpec(a.shape, lambda i: (0,) * a.ndim, pipeline_mode=pl.Buffered(1))
    mods = [_mod_spec(mod, c, tpg) for c in (1, 0, 2, 4, 3, 5)]
    in_specs = ([tok(D_MODEL), tok(POOL_WIDTH), tok(ATTN_WIDTH)] + mods
                + [full(a) for a in gains] + [full(a) for a in weights])
    operands = [x, pool_in, attn, *([mod] * 6), *gains, *weights]
    out_specs = [tok(D_MODEL)]
    out_shape = [jax.ShapeDtypeStruct((n_tok, D_MODEL), F32)]
    scratch = [pltpu.VMEM((HIST_PAD + tm, POOL_WIDTH), F32)] if pool_in_kernel else []
    stream = None
    if cache is not None:
        page_table, cacheT_k = cache
        depth = cacheT_k.shape[0]
        n_seq, n_pages = page_table.shape
        pages_per_step, rem = divmod(depth * n_seq * n_pages, n_steps)
        assert rem == 0 and n_pages % pages_per_step == 0 and pages_per_step % (STREAM_CHUNKS * PAGES_PER_BLOCK) == 0
        assert n_pages // PAGES_PER_BLOCK <= PAGE_SIZE
        stream = dict(n_seq=n_seq, n_pages=n_pages, pages_per_step=pages_per_step)
        in_specs += [pl.BlockSpec(memory_space=pltpu.SMEM), pl.BlockSpec(memory_space=pl.ANY)]
        operands += [page_table, cacheT_k]
        unit = lambda i: (i * pages_per_step) // n_pages
        out_specs.append(pl.BlockSpec((1, 1, ATTN_WIDTH, PAGE_SIZE), lambda i: (unit(i) // n_seq, unit(i) % n_seq, 0, 0)))
        out_shape.append(jax.ShapeDtypeStruct((depth, n_seq, ATTN_WIDTH, PAGE_SIZE), F32))
        scratch += [pltpu.VMEM((STREAM_CHUNKS, pages_per_step // STREAM_CHUNKS, ATTN_WIDTH, PAGE_SIZE), F32),
                    pltpu.SemaphoreType.DMA((STREAM_CHUNKS,)),
                    pltpu.VMEM((ATTN_WIDTH, PAGE_SIZE), F32)]
    outs = pl.pallas_call(
        functools.partial(_tail_kernel, pool_in_kernel=pool_in_kernel, tiles_per_seq=tpg, stream=stream),
        grid=(n_steps,),
        in_specs=in_specs,
        out_specs=out_specs,
        out_shape=out_shape,
        scratch_shapes=scratch,
        compiler_params=_cparams("arbitrary"),
        name="tail",
    )(*operands)
    return outs if cache is not None else outs[0]


def _select_kernel(q_ref, km_ref, o_ref, *, n_blocks):
    sc = jnp.dot(q_ref[0], km_ref[0, 0], preferred_element_type=F32, precision=lax.Precision.HIGHEST)
    lane = lax.broadcasted_iota(jnp.int32, sc.shape, 1)
    cur = jnp.where(lane < n_blocks, sc, -jnp.inf)
    out = jnp.zeros(sc.shape, jnp.int32)
    for r in range(MOBA_TOPK):
        m = jnp.max(cur, axis=1, keepdims=True)
        pick = jnp.min(jnp.where(cur == m, lane, PAGE_SIZE), axis=1, keepdims=True)
        out = jnp.where(lane == r, pick, out)
        cur = jnp.where(lane == pick, -jnp.inf, cur)
    o_ref[0] = out


def _select_sample(q_exp, kmeanT, layer, n_blocks):
    n_seq, rows, _ = q_exp.shape
    return pl.pallas_call(
        functools.partial(_select_kernel, n_blocks=n_blocks),
        grid=(n_seq,),
        in_specs=[pl.BlockSpec((1, rows, ATTN_WIDTH), lambda b: (b, 0, 0)),
                  pl.BlockSpec((1, 1, ATTN_WIDTH, PAGE_SIZE), lambda b: (layer, b, 0, 0))],
        out_specs=pl.BlockSpec((1, rows, PAGE_SIZE), lambda b: (b, 0, 0)),
        out_shape=jax.ShapeDtypeStruct((n_seq, rows, PAGE_SIZE), jnp.int32),
        compiler_params=_cparams("arbitrary"),
        name="select_sample",
    )(q_exp, kmeanT)


def _attend_kernel(idx_ref, pt_ref, qT_ref, kT_ref, vT_ref, ck_ref, cv_ref, o_ref, kbuf_ref, vbuf_ref, sem_ref,
                   *, layer, n_new):
    n_seq, n_pages = pt_ref.shape
    tiles_per_tok = MOBA_TOPK * PAGES_PER_BLOCK
    tiles_per_head = n_new * tiles_per_tok
    b = pl.program_id(0)
    slot = b % 2

    def tile_copies(hh, phys, sl, j):
        rows = pl.ds(pl.multiple_of(hh * HEAD_DIM, HEAD_DIM), HEAD_DIM)
        return (pltpu.make_async_copy(ck_ref.at[layer, phys, rows, :], kbuf_ref.at[sl, j], sem_ref.at[sl, 0]),
                pltpu.make_async_copy(cv_ref.at[layer, phys, rows, :], vbuf_ref.at[sl, j], sem_ref.at[sl, 1]))

    def issue(bb, sl):
        def per_head(hh, carry):
            base = (bb * N_HEADS + hh) * (n_new * MOBA_TOPK)
            for t in range(n_new):
                for r in range(MOBA_TOPK):
                    blk = idx_ref[base + t * MOBA_TOPK + r]
                    for pg in range(PAGES_PER_BLOCK):
                        logical = jnp.minimum(blk * PAGES_PER_BLOCK + pg, n_pages - 1)
                        j = hh * tiles_per_head + (t * MOBA_TOPK + r) * PAGES_PER_BLOCK + pg
                        ck, cv = tile_copies(hh, pt_ref[bb, logical], sl, j)
                        ck.start()
                        cv.start()
            return carry

        lax.fori_loop(0, N_HEADS, per_head, 0)

    @pl.when(b == 0)
    def _():
        issue(b, slot)
        o_ref[...] = jnp.zeros(o_ref.shape, F32)

    @pl.when(b + 1 < pl.num_programs(0))
    def _():
        issue(b + 1, 1 - slot)

    for j in range(N_HEADS * tiles_per_head):
        ck, cv = tile_copies(0, 0, slot, j)
        ck.wait()
        cv.wait()

    lane = lax.broadcasted_iota(jnp.int32, (1, qT_ref.shape[1]), 1)
    for hh in range(N_HEADS):
        rows = slice(hh * HEAD_DIM, (hh + 1) * HEAD_DIM)
        qT = qT_ref[rows, :]
        k_new = kT_ref[rows, :]
        v_new = vT_ref[rows, :]
        out = o_ref[rows, :]
        for t in range(n_new):
            col = b * n_new + t
            first = hh * tiles_per_head + t * tiles_per_tok
            qcol = jnp.sum(jnp.where(lane == col, qT, 0.0), axis=1, keepdims=True) * SM_SCALE
            scores = [jnp.sum(kbuf_ref[slot, first + j] * qcol, axis=0, keepdims=True) for j in range(tiles_per_tok)]
            own_ok = (lane >= b * n_new) & (lane <= col)
            scores.append(jnp.where(own_ok, jnp.sum(k_new * qcol, axis=0, keepdims=True), NEG_INF))
            m = scores[0]
            for s in scores[1:]:
                m = jnp.maximum(m, s)
            m = jnp.max(m, axis=1, keepdims=True)
            probs = [jnp.exp(s - m) for s in scores]
            tot = probs[0]
            for p in probs[1:]:
                tot = tot + p
            denom = jnp.sum(tot, axis=1, keepdims=True)
            pv = v_new * probs[-1]
            for j in range(tiles_per_tok):
                pv = pv + vbuf_ref[slot, first + j] * probs[j]
            o_col = jnp.sum(pv, axis=1, keepdims=True) / denom
            out = jnp.where(lane == col, o_col, out)
        o_ref[rows, :] = out


def _attend_sample(idx_flat, page_table, qT, kT, vT, cacheT_k, cacheT_v, layer, n_new):
    n_seq = page_table.shape[0]
    n_tok = qT.shape[1]
    n_tiles = N_HEADS * n_new * MOBA_TOPK * PAGES_PER_BLOCK
    whole = pl.BlockSpec((ATTN_WIDTH, n_tok), lambda b, idx, pt: (0, 0))
    hbm = pl.BlockSpec(memory_space=pl.ANY)
    return pl.pallas_call(
        functools.partial(_attend_kernel, layer=layer, n_new=n_new),
        grid_spec=pltpu.PrefetchScalarGridSpec(
            num_scalar_prefetch=2,
            grid=(n_seq,),
            in_specs=[whole, whole, whole, hbm, hbm],
            out_specs=whole,
            scratch_shapes=[pltpu.VMEM((2, n_tiles, HEAD_DIM, PAGE_SIZE), F32),
                            pltpu.VMEM((2, n_tiles, HEAD_DIM, PAGE_SIZE), F32),
                            pltpu.SemaphoreType.DMA((2, 2))]),
        out_shape=jax.ShapeDtypeStruct((ATTN_WIDTH, n_tok), F32),
        compiler_params=_cparams("arbitrary"),
        name="attend_sample",
    )(idx_flat, page_table, qT, kT, vT, cacheT_k, cacheT_v)


def kernel(x_prompt, x_sample, cache_k, cache_v, state_pool, page_table, c_prompt, c_sample, w_ada, b_ada, g_mix_pre, g_mix_post, w_in, pool_lin, pool_scale, w_branch_pool, w_branch_attn, w_out, g_ffn_pre, g_ffn_post, w_ff1, w_ff2):
    depth = w_ada.shape[0]
    bp, sp, _ = x_prompt.shape
    bs, ts, _ = x_sample.shape
    n_pages = page_table.shape[1]
    n_blocks = n_pages // PAGES_PER_BLOCK
    n_phys = cache_k.shape[1]

    cacheT_k = jnp.transpose(cache_k, (0, 1, 3, 4, 2)).reshape(depth, n_phys, ATTN_WIDTH, PAGE_SIZE)
    cacheT_v = jnp.transpose(cache_v, (0, 1, 3, 4, 2)).reshape(depth, n_phys, ATTN_WIDTH, PAGE_SIZE)

    ada = _ada(jnp.concatenate([c_prompt, c_sample], axis=0), w_ada, b_ada)

    head_mask = (jnp.arange(ATTN_WIDTH)[None, :] // HEAD_DIM == jnp.arange(N_HEADS)[:, None]).astype(F32)

    xp = x_prompt.reshape(bp * sp, D_MODEL)
    xs = x_sample.reshape(bs * ts, D_MODEL)
    kp_l, vp_l, hp_l, ks_l, vs_l, hs_l = [], [], [], [], [], []
    for l in range(depth):
        wu = w_in[l, :, :POOL_WIDTH].astype(BF16)
        wqkvT = w_in[l, :, POOL_WIDTH:POOL_WIDTH + 3 * ATTN_WIDTH].T.astype(BF16)
        wg = w_in[l, :, POOL_WIDTH + 3 * ATTN_WIDTH:].astype(BF16)
        lin = pool_lin[l].astype(BF16)
        ps = pool_scale[l].reshape(1, POOL_WIDTH)
        wbp = w_branch_pool[l].astype(BF16)
        wba = w_branch_attn[l].astype(BF16)
        wo = w_out[l].astype(BF16)
        w1 = w_ff1[l].astype(BF16)
        w2 = w_ff2[l].astype(BF16)
        gpre = g_mix_pre[l].reshape(1, D_MODEL)
        gpost = g_mix_post[l].reshape(1, D_MODEL)
        gfpre = g_ffn_pre[l].reshape(1, D_MODEL)
        gfpost = g_ffn_post[l].reshape(1, D_MODEL)
        gains = (gpre, gpost, gfpre, gfpost)
        weights = (wg, lin, ps, wbp, wba, wo, w1, w2)
        mod_p = ada[l, :bp].reshape(bp, 1, ADA_CHUNKS * D_MODEL)
        mod_s = jnp.repeat(ada[l, bp:], ts, axis=0).reshape(1, bs * ts, ADA_CHUNKS * D_MODEL)

        u, qT, kT, vT = _inproj(xp, mod_p, gpre, wu, wqkvT, tm=TOKEN_TILE, seq=sp)
        attn = _moba_prompt(qT, kT, vT).reshape(bp * sp, ATTN_WIDTH)
        if l == 0:
            xp, kmeanT = _tail(xp, u, attn, mod_p, gains, weights, tm=TOKEN_TILE, seq=sp, pool_in_kernel=True,
                               cache=(page_table, cacheT_k))
        else:
            xp = _tail(xp, u, attn, mod_p, gains, weights, tm=TOKEN_TILE, seq=sp, pool_in_kernel=True)
        kp_l.append(kT)
        vp_l.append(vT)
        hp_l.append(u.reshape(bp, sp, POOL_WIDTH)[:, sp - POOL_HIST:, :])

        n_s = bs * ts
        u_s, qT_s, kT_s, vT_s = _inproj(xs, mod_s, gpre, wu, wqkvT, tm=n_s, seq=n_s)
        qT_s, kT_s, vT_s = qT_s[0], kT_s[0], vT_s[0]
        z = jnp.concatenate([state_pool[l], u_s.reshape(bs, ts, POOL_WIDTH)], axis=1)
        d_s = _pool_sample(jnp.transpose(z, (1, 0, 2)), ts)
        d_s = jnp.transpose(d_s, (1, 0, 2)).reshape(n_s, POOL_WIDTH)
        q_s = qT_s.T.reshape(bs, 1, ts, ATTN_WIDTH)
        q_exp = (q_s * head_mask[None, :, None, :]).reshape(bs, N_HEADS * ts, ATTN_WIDTH)
        idx = _select_sample(q_exp, kmeanT, l, n_blocks)[:, :, :MOBA_TOPK]
        attnT_s = _attend_sample(idx.reshape(-1), page_table, qT_s, kT_s, vT_s, cacheT_k, cacheT_v, l, ts)
        attn_s = attnT_s.T.astype(BF16)
        xs = _tail(xs, d_s, attn_s, mod_s, gains, weights, tm=n_s, seq=n_s, pool_in_kernel=False)
        ks_l.append(kT_s.T.reshape(bs, ts, N_HEADS, HEAD_DIM))
        vs_l.append(vT_s.T.reshape(bs, ts, N_HEADS, HEAD_DIM))
        hs_l.append(z[:, ts:, :])

    def untranspose(parts):
        t = jnp.stack(parts).reshape(depth, bp, N_HEADS, HEAD_DIM, sp)
        return jnp.transpose(t, (0, 1, 4, 2, 3))

    return (xp.reshape(bp, sp, D_MODEL), xs.reshape(bs, ts, D_MODEL),
            untranspose(kp_l), untranspose(vp_l), jnp.stack(hp_l),
            jnp.stack(ks_l), jnp.stack(vs_l), jnp.stack(hs_l))
```

```python
import functools

import jax
import jax.numpy as jnp
from jax import lax
from jax.experimental import pallas as pl
from jax.experimental.pallas import tpu as pltpu

F32 = jnp.float32
BF16 = jnp.bfloat16

D_MODEL = 1024
N_HEADS = 8
HEAD_DIM = 64
ATTN_WIDTH = N_HEADS * HEAD_DIM
MOBA_BLOCK = 256
MOBA_TOPK = 3
PAGE_SIZE = 128
PAGES_PER_BLOCK = MOBA_BLOCK // PAGE_SIZE
POOL_WINDOWS = (2, 4, 8, 16)
POOL_WIDTH = 512
POOL_GROUP_WIDTH = 128
POOL_HIST = 15
HIST_PAD = 16
D_FF = 4 * D_MODEL
ADA_CHUNKS = 6
EPS = 1e-6
NEG_INF = -1e30
SM_SCALE = HEAD_DIM ** -0.5
LOG2_E = 1.4426950408889634
V_ROWS = HEAD_DIM + 16

V7X_VMEM_LIMIT = 56 * 1024 * 1024

TOKEN_TILE = 512
STREAM_TILE = 256
SUB_TILE = 256
STREAM_CHUNKS = 4


def _cparams(*sem):
    return pltpu.CompilerParams(dimension_semantics=sem, vmem_limit_bytes=V7X_VMEM_LIMIT)


def _rms(x, g):
    return x * lax.rsqrt(jnp.mean(x * x, axis=-1, keepdims=True) + EPS) * g


def _sigmoid(x):
    return 1.0 / (1.0 + jnp.exp(-x))


def _bdot(a, b):
    return jnp.dot(a, b, preferred_element_type=F32)


def _ada_kernel(c_ref, w_ref, b_ref, o_ref):
    o_ref[0] = _bdot(c_ref[...].astype(BF16), w_ref[0].astype(BF16)) + b_ref[0]


def _ada(c_all, w_ada, b_ada):
    depth, _, width = w_ada.shape
    n = c_all.shape[0]
    tn = 1536
    return pl.pallas_call(
        _ada_kernel,
        grid=(depth, width // tn),
        in_specs=[pl.BlockSpec((n, D_MODEL), lambda l, j: (0, 0)),
                  pl.BlockSpec((1, D_MODEL, tn), lambda l, j: (l, 0, j)),
                  pl.BlockSpec((1, 1, tn), lambda l, j: (l, 0, j))],
        out_specs=pl.BlockSpec((1, n, tn), lambda l, j: (l, 0, j)),
        out_shape=jax.ShapeDtypeStruct((depth, n, width), F32),
        compiler_params=_cparams("arbitrary", "arbitrary"),
        name="ada",
    )(c_all, w_ada, b_ada.reshape(depth, 1, width))


def _mod_spec(mod, chunk, tiles_per_group):
    rows = mod.shape[1]
    return pl.BlockSpec((1, rows, D_MODEL), lambda i: (i // tiles_per_group, 0, chunk))


def _modulated(x_ref, g_ref, sc_ref, sh_ref):
    return (_rms(x_ref[...], g_ref[...]) * (1.0 + sc_ref[0]) + sh_ref[0]).astype(BF16)


def _inproj_kernel(x_ref, sc_ref, sh_ref, g_ref, wu_ref, wqkv_ref, u_ref, qT_ref, kT_ref, vT_ref):
    h = _modulated(x_ref, g_ref, sc_ref, sh_ref)
    u_ref[...] = _bdot(h, wu_ref[...])
    qkvT = lax.dot_general(wqkv_ref[...], h, (((1,), (1,)), ((), ())), preferred_element_type=F32)
    qT_ref[0] = qkvT[0:ATTN_WIDTH]
    kT_ref[0] = qkvT[ATTN_WIDTH:2 * ATTN_WIDTH]
    vT_ref[0] = qkvT[2 * ATTN_WIDTH:3 * ATTN_WIDTH]


def _inproj(x, mod, g, wu, wqkvT, *, tm, seq):
    n_tok = x.shape[0]
    tpg = seq // tm
    n_seq = n_tok // seq
    const = lambda i: (0, 0)
    tok = lambda w: pl.BlockSpec((tm, w), lambda i: (i, 0))
    tspec = pl.BlockSpec((1, ATTN_WIDTH, tm), lambda i: (i // tpg, 0, i % tpg))
    tshape = jax.ShapeDtypeStruct((n_seq, ATTN_WIDTH, seq), F32)
    return pl.pallas_call(
        _inproj_kernel,
        grid=(n_tok // tm,),
        in_specs=[tok(D_MODEL), _mod_spec(mod, 1, tpg), _mod_spec(mod, 0, tpg),
                  pl.BlockSpec((1, D_MODEL), const),
                  pl.BlockSpec(wu.shape, const), pl.BlockSpec(wqkvT.shape, const)],
        out_specs=[tok(POOL_WIDTH), tspec, tspec, tspec],
        out_shape=[jax.ShapeDtypeStruct((n_tok, POOL_WIDTH), F32), tshape, tshape, tshape],
        compiler_params=_cparams("arbitrary"),
        name="inproj",
    )(x, mod, mod, g, wu, wqkvT)


def _pool_stage(u_ref, z_ref, tile_in_seq):
    tm = u_ref.shape[0]

    @pl.when(tile_in_seq == 0)
    def _():
        z_ref[0:HIST_PAD, :] = jnp.zeros((HIST_PAD, POOL_WIDTH), F32)

    @pl.when(tile_in_seq > 0)
    def _():
        z_ref[0:HIST_PAD, :] = z_ref[tm:tm + HIST_PAD, :]

    z_ref[HIST_PAD:HIST_PAD + tm, :] = u_ref[...]


def _pool_rows(z_ref, row0, n_rows, pos0):
    pos = pos0 + lax.broadcasted_iota(jnp.int32, (n_rows, POOL_GROUP_WIDTH), 0)
    parts = []
    for g, w in enumerate(POOL_WINDOWS):
        cols = slice(g * POOL_GROUP_WIDTH, (g + 1) * POOL_GROUP_WIDTH)
        base = HIST_PAD + row0
        cur = z_ref[base:base + n_rows, cols]
        acc = cur
        for r in range(1, w):
            acc = acc + z_ref[base - r:base - r + n_rows, cols]
        cnt = jnp.minimum(pos + 1, w).astype(F32)
        parts.append((acc / cnt - cur).astype(BF16))
    return parts


def _pool_sample_kernel(z_ref, d_ref):
    n_new = d_ref.shape[0]
    for t in range(n_new):
        for g, w in enumerate(POOL_WINDOWS):
            cols = slice(g * POOL_GROUP_WIDTH, (g + 1) * POOL_GROUP_WIDTH)
            cur = z_ref[POOL_HIST + t, :, cols]
            acc = cur
            for r in range(1, w):
                acc = acc + z_ref[POOL_HIST + t - r, :, cols]
            d_ref[t, :, cols] = (acc / float(w) - cur).astype(BF16)


def _pool_sample(z_tm, n_new):
    _, bd, _ = z_tm.shape
    return pl.pallas_call(
        _pool_sample_kernel,
        out_shape=jax.ShapeDtypeStruct((n_new, bd, POOL_WIDTH), BF16),
        name="pool_sample",
    )(z_tm)


def _moba_kernel(qT_ref, kT_ref, vT_ref, o_ref, kb_ref, vb_ref, qm_ref, sel_ref, st_ref, p_ref):
    hp = 2 * HEAD_DIM
    s_len = qT_ref.shape[2]
    nb = s_len // MOBA_BLOCK
    qT = qT_ref[0]
    k = kT_ref[0].T
    kmean = jnp.sum(k.reshape(nb, MOBA_BLOCK, hp), axis=1) * (1.0 / MOBA_BLOCK)
    kb_ref[...] = k.astype(BF16)
    ones = jnp.ones((V_ROWS - HEAD_DIM, s_len), BF16)
    for hh in range(2):
        vb_ref[hh, 0:HEAD_DIM, :] = vT_ref[0, hh * HEAD_DIM:(hh + 1) * HEAD_DIM, :].astype(BF16)
        vb_ref[hh, HEAD_DIM:V_ROWS, :] = ones

    row_head = lax.broadcasted_iota(jnp.int32, (hp, s_len), 0) // HEAD_DIM
    lane_head = lax.broadcasted_iota(jnp.int32, (nb, hp), 1) // HEAD_DIM
    blk = lax.broadcasted_iota(jnp.int32, (nb, s_len), 0)
    n_past = lax.broadcasted_iota(jnp.int32, (nb, s_len), 1) // MOBA_BLOCK
    valid = blk < n_past
    for hh in range(2):
        qm = jnp.where(row_head == hh, qT, 0.0)
        for i in range(nb):
            qm_ref[hh, i] = (qm[:, i * MOBA_BLOCK:(i + 1) * MOBA_BLOCK] * (SM_SCALE * LOG2_E)).astype(BF16)
        sc = jnp.dot(jnp.where(lane_head == hh, kmean, 0.0), qT, preferred_element_type=F32,
                     precision=lax.Precision.HIGHEST)
        sc = jnp.where(valid, sc, NEG_INF)
        rank = jnp.zeros((nb, s_len), F32)
        for m in range(nb):
            row = sc[m:m + 1, :]
            beats = (row > sc) | ((row == sc) & (blk > m))
            rank = rank + jnp.where(beats, 1.0, 0.0)
        sel_ref[hh] = jnp.where(valid & (rank < float(MOBA_TOPK)), 1.0, 0.0)

    key_pos = lax.broadcasted_iota(jnp.int32, (MOBA_BLOCK, MOBA_BLOCK), 0)
    q_pos = lax.broadcasted_iota(jnp.int32, (MOBA_BLOCK, MOBA_BLOCK), 1)
    causal = key_pos <= q_pos
    units = [(i, hh) for i in range(nb) for hh in range(2)]
    state = {}

    def blocks(j):
        return slice(j * MOBA_BLOCK, (j + 1) * MOBA_BLOCK)

    def scores(n):
        i, hh = units[n]
        picked = [sel_ref[hh, j:j + 1, blocks(i)] > 0.0 for j in range(i)]
        st_ref[n % 2, 0:(i + 1) * MOBA_BLOCK, :] = _bdot(kb_ref[0:(i + 1) * MOBA_BLOCK, :], qm_ref[hh, i])
        m_run = jnp.max(jnp.where(causal, st_ref[n % 2, blocks(i), :], NEG_INF), axis=0, keepdims=True)
        for j in range(i):
            cm = jnp.max(st_ref[n % 2, blocks(j), :], axis=0, keepdims=True)
            m_run = jnp.maximum(m_run, jnp.where(picked[j], cm, NEG_INF))
        state[n] = (m_run, picked)

    def values(n):
        i, hh = units[n]
        m_run, picked = state.pop(n)
        for j in range(i + 1):
            st = st_ref[n % 2, blocks(j), :]
            if j == i:
                st = jnp.where(causal, st, NEG_INF)
                shift = m_run
            else:
                shift = jnp.where(picked[j], m_run, -NEG_INF)
            p_ref[n % 2, blocks(j), :] = jnp.exp2(st - shift).astype(BF16)
        acc = _bdot(vb_ref[hh, :, 0:(i + 1) * MOBA_BLOCK], p_ref[n % 2, 0:(i + 1) * MOBA_BLOCK, :])
        state[("out", i, hh)] = acc[0:HEAD_DIM] / acc[HEAD_DIM:HEAD_DIM + 1]
        if hh == 1:
            both = jnp.concatenate([state.pop(("out", i, 0)), state.pop(("out", i, 1))], axis=0)
            o_ref[0, blocks(i), :] = both.T.astype(BF16)

    for n in range(len(units) + 1):
        if n < len(units):
            scores(n)
        if n > 0:
            values(n - 1)


def _moba_prompt(qT, kT, vT):
    b, _, s = qT.shape
    hp = 2 * HEAD_DIM
    nb = s // MOBA_BLOCK
    spec = pl.BlockSpec((1, hp, s), lambda i, p: (i, p, 0))
    return pl.pallas_call(
        _moba_kernel,
        grid=(b, ATTN_WIDTH // hp),
        in_specs=[spec, spec, spec],
        out_specs=pl.BlockSpec((1, s, hp), lambda i, p: (i, 0, p)),
        out_shape=jax.ShapeDtypeStruct((b, s, ATTN_WIDTH), BF16),
        scratch_shapes=[pltpu.VMEM((s, hp), BF16),
                        pltpu.VMEM((2, V_ROWS, s), BF16),
                        pltpu.VMEM((2, nb, hp, MOBA_BLOCK), BF16),
                        pltpu.VMEM((2, nb, s), F32),
                        pltpu.VMEM((2, s, MOBA_BLOCK), F32),
                        pltpu.VMEM((2, s, MOBA_BLOCK), BF16)],
        compiler_params=_cparams("arbitrary", "arbitrary"),
        name="moba_prompt",
    )(qT, kT, vT)


def _cache_stream(pt_ref, ck_ref, buf_ref, sem_ref, *, n_seq, n_pages, pages_per_step):
    cp = pages_per_step // STREAM_CHUNKS

    def page_copy(layer, phys, k, p):
        return pltpu.make_async_copy(ck_ref.at[layer, phys], buf_ref.at[k, p], sem_ref.at[k])

    def issue(st, k):
        first = st * pages_per_step + k * cp
        layer = first // (n_seq * n_pages)
        seq = (first // n_pages) % n_seq
        page0 = first % n_pages
        for p in range(cp):
            page_copy(layer, pt_ref[seq, page0 + p], k, p).start()

    def point(step, k, acc):
        for p in range(cp):
            page_copy(0, 0, k, p).wait()
        if k == 0:
            issue(step, STREAM_CHUNKS - 1)
        else:
            @pl.when(step + 1 < pl.num_programs(0))
            def _():
                issue(step + 1, k - 1)
        lane = lax.broadcasted_iota(jnp.int32, (ATTN_WIDTH, PAGE_SIZE), 1)
        block0 = ((step * pages_per_step + k * cp) % n_pages) // PAGES_PER_BLOCK
        for b2 in range(cp // PAGES_PER_BLOCK):
            t = buf_ref[k, PAGES_PER_BLOCK * b2]
            for e in range(1, PAGES_PER_BLOCK):
                t = t + buf_ref[k, PAGES_PER_BLOCK * b2 + e]
            acc = jnp.where(lane == block0 + b2, jnp.sum(t, axis=1, keepdims=True), acc)
        return acc

    return issue, point


def _tail_kernel(*refs, pool_in_kernel, tiles_per_seq, stream, stack, n_steps):
    (x_ref, p_ref, at_ref, sc1_ref, sh1_ref, gt1_ref, sc2_ref, sh2_ref, gt2_ref,
     gpre_ref, gpost_ref, gfpre_ref, gfpost_ref, wg_ref, lin_ref, ps_ref, wbp_ref, wba_ref, wout_ref,
     w1_ref, w2_ref) = refs[:21]
    rest = list(refs[21:])
    if stream is not None:
        pt_ref, ck_ref = rest[:2]
        rest = rest[2:]
    n_groups, per_group = stack
    n_src = n_groups * per_group
    stack_src, rest = rest[:n_src], rest[n_src:]
    o_ref = rest.pop(0)
    if stream is not None:
        km_ref = rest.pop(0)
    stack_dst, rest = rest[:n_groups], rest[n_groups:]
    z_ref = rest.pop(0) if pool_in_kernel else None
    stack_sem = rest.pop() if n_src else None
    step = pl.program_id(0)

    def stack_copies():
        copies = []
        for n, s in enumerate(stack_src):
            g, l = divmod(n, per_group)
            rows_per_step = s.shape[0] // n_steps
            first = pl.multiple_of(step * rows_per_step, 8)
            copies.append(pltpu.make_async_copy(s.at[pl.ds(first, rows_per_step), :],
                                                stack_dst[g].at[l, pl.ds(first, rows_per_step), :],
                                                stack_sem.at[n]))
        return copies

    for cp in stack_copies():
        cp.start()

    def stream_point(k):
        pass

    if stream is not None:
        buf_ref, sem_ref, kacc_ref = rest
        issue, point = _cache_stream(pt_ref, ck_ref, buf_ref, sem_ref, **stream)

        @pl.when(step == 0)
        def _():
            for k in range(STREAM_CHUNKS - 1):
                issue(step, k)

        @pl.when((step * stream["pages_per_step"]) % stream["n_pages"] == 0)
        def _():
            kacc_ref[...] = jnp.zeros(kacc_ref.shape, F32)

        def stream_point(k):
            acc = point(step, k, kacc_ref[...])
            kacc_ref[...] = acc
            if k == STREAM_CHUNKS - 1:
                km_ref[0, 0] = acc * (1.0 / MOBA_BLOCK)

    tm = x_ref.shape[0]
    sub = min(tm, SUB_TILE)
    tile_in_seq = step % tiles_per_seq
    if pool_in_kernel:
        _pool_stage(p_ref, z_ref, tile_in_seq)

    def mod_rows(ref, rows):
        return ref[0] if ref.shape[1] == 1 else ref[0, rows, :]

    for s in range(tm // sub):
        rows = slice(s * sub, (s + 1) * sub)
        here = stream_point if s == 0 else (lambda k: None)
        here(0)
        x = x_ref[rows, :]
        h = (_rms(x, gpre_ref[...]) * (1.0 + mod_rows(sc1_ref, rows)) + mod_rows(sh1_ref, rows)).astype(BF16)
        gate_pool = _sigmoid(_bdot(h, wg_ref[:, 0:D_MODEL]))
        gate_attn = _sigmoid(_bdot(h, wg_ref[:, D_MODEL:2 * D_MODEL]))
        if pool_in_kernel:
            d_parts = _pool_rows(z_ref, s * sub, sub, tile_in_seq * tm + s * sub)
        else:
            d_parts = [p_ref[rows, g * POOL_GROUP_WIDTH:(g + 1) * POOL_GROUP_WIDTH] for g in range(len(POOL_WINDOWS))]
        mixed = jnp.concatenate([_bdot(d_parts[g], lin_ref[g]) for g in range(len(POOL_WINDOWS))],
                                axis=-1) * ps_ref[...]
        merged = (gate_pool * _bdot(mixed.astype(BF16), wbp_ref[...])
                  + gate_attn * _bdot(at_ref[rows, :], wba_ref[...]))
        here(1)
        x1 = x + mod_rows(gt1_ref, rows) * _rms(_bdot(merged.astype(BF16), wout_ref[...]), gpost_ref[...])
        h2 = (_rms(x1, gfpre_ref[...]) * (1.0 + mod_rows(sc2_ref, rows)) + mod_rows(sh2_ref, rows)).astype(BF16)
        here(2)
        ff = jnp.zeros(x.shape, F32)
        for c in range(D_FF // D_MODEL):
            cols = slice(c * D_MODEL, (c + 1) * D_MODEL)
            a = jnp.maximum(_bdot(h2, w1_ref[:, cols]), 0.0)
            ff = ff + _bdot((a * a).astype(BF16), w2_ref[cols, :])
            if c == 1:
                here(3)
        o_ref[rows, :] = x1 + mod_rows(gt2_ref, rows) * _rms(ff, gfpost_ref[...])

    for cp in stack_copies():
        cp.wait()


def _tail(x, pool_in, attn, mod, gains, weights, *, tm, seq, pool_in_kernel, cache=None, stack=()):
    n_tok = x.shape[0]
    n_steps = n_tok // tm
    tpg = seq // tm
    tok = lambda w: pl.BlockSpec((tm, w), lambda i: (i, 0))
    full = lambda a: pl.BlockSpec(a.shape, lambda i: (0,) * a.ndim, pipeline_mode=pl.Buffered(1))
    mods = [_mod_spec(mod, c, tpg) for c in (1, 0, 2, 4, 3, 5)]
    in_specs = ([tok(D_MODEL), tok(POOL_WIDTH), tok(ATTN_WIDTH)] + mods
                + [full(a) for a in gains] + [full(a) for a in weights])
    operands = [x, pool_in, attn, *([mod] * 6), *gains, *weights]
    out_specs = [tok(D_MODEL)]
    out_shape = [jax.ShapeDtypeStruct((n_tok, D_MODEL), F32)]
    scratch = [pltpu.VMEM((HIST_PAD + tm, POOL_WIDTH), F32)] if pool_in_kernel else []
    stream = None
    if cache is not None:
        page_table, cacheT_k = cache
        depth = cacheT_k.shape[0]
        n_seq, n_pages = page_table.shape
        pages_per_step, rem = divmod(depth * n_seq * n_pages, n_steps)
        assert rem == 0 and n_pages % pages_per_step == 0 and pages_per_step % (STREAM_CHUNKS * PAGES_PER_BLOCK) == 0
        assert n_pages // PAGES_PER_BLOCK <= PAGE_SIZE
        stream = dict(n_seq=n_seq, n_pages=n_pages, pages_per_step=pages_per_step)
        in_specs += [pl.BlockSpec(memory_space=pltpu.SMEM), pl.BlockSpec(memory_space=pl.ANY)]
        operands += [page_table, cacheT_k]
        unit = lambda i: (i * pages_per_step) // n_pages
        out_specs.append(pl.BlockSpec((1, 1, ATTN_WIDTH, PAGE_SIZE), lambda i: (unit(i) // n_seq, unit(i) % n_seq, 0, 0)))
        out_shape.append(jax.ShapeDtypeStruct((depth, n_seq, ATTN_WIDTH, PAGE_SIZE), F32))
        scratch += [pltpu.VMEM((STREAM_CHUNKS, pages_per_step // STREAM_CHUNKS, ATTN_WIDTH, PAGE_SIZE), F32),
                    pltpu.SemaphoreType.DMA((STREAM_CHUNKS,)),
                    pltpu.VMEM((ATTN_WIDTH, PAGE_SIZE), F32)]
    per_group = len(stack[0]) if stack else 0
    if stack:
        hbm = pl.BlockSpec(memory_space=pl.ANY)
        flat = [a for group in stack for a in group]
        assert all(len(g) == per_group for g in stack) and all(a.shape[0] % (8 * n_steps) == 0 for a in flat)
        in_specs += [hbm] * len(flat)
        operands += flat
        out_specs += [hbm] * len(stack)
        out_shape += [jax.ShapeDtypeStruct((per_group,) + g[0].shape, g[0].dtype) for g in stack]
        scratch += [pltpu.SemaphoreType.DMA((len(flat),))]
    outs = pl.pallas_call(
        functools.partial(_tail_kernel, pool_in_kernel=pool_in_kernel, tiles_per_seq=tpg, stream=stream,
                          stack=(len(stack), per_group), n_steps=n_steps),
        grid=(n_steps,),
        in_specs=in_specs,
        out_specs=out_specs,
        out_shape=out_shape,
        scratch_shapes=scratch,
        compiler_params=_cparams("arbitrary"),
        name="tail",
    )(*operands)
    return outs


def _select_kernel(q_ref, km_ref, o_ref, *, n_blocks):
    sc = jnp.dot(q_ref[0], km_ref[0, 0], preferred_element_type=F32, precision=lax.Precision.HIGHEST)
    lane = lax.broadcasted_iota(jnp.int32, sc.shape, 1)
    cur = jnp.where(lane < n_blocks, sc, -jnp.inf)
    out = jnp.zeros(sc.shape, jnp.int32)
    for r in range(MOBA_TOPK):
        m = jnp.max(cur, axis=1, keepdims=True)
        pick = jnp.min(jnp.where(cur == m, lane, PAGE_SIZE), axis=1, keepdims=True)
        out = jnp.where(lane == r, pick, out)
        cur = jnp.where(lane == pick, -jnp.inf, cur)
    o_ref[0] = out


def _select_sample(q_exp, kmeanT, layer, n_blocks):
    n_seq, rows, _ = q_exp.shape
    return pl.pallas_call(
        functools.partial(_select_kernel, n_blocks=n_blocks),
        grid=(n_seq,),
        in_specs=[pl.BlockSpec((1, rows, ATTN_WIDTH), lambda b: (b, 0, 0)),
                  pl.BlockSpec((1, 1, ATTN_WIDTH, PAGE_SIZE), lambda b: (layer, b, 0, 0))],
        out_specs=pl.BlockSpec((1, rows, PAGE_SIZE), lambda b: (b, 0, 0)),
        out_shape=jax.ShapeDtypeStruct((n_seq, rows, PAGE_SIZE), jnp.int32),
        compiler_params=_cparams("arbitrary"),
        name="select_sample",
    )(q_exp, kmeanT)


def _attend_kernel(idx_ref, pt_ref, qT_ref, kT_ref, vT_ref, ck_ref, cv_ref, o_ref, kbuf_ref, vbuf_ref, sem_ref,
                   *, layer, n_new):
    n_seq, n_pages = pt_ref.shape
    tiles_per_tok = MOBA_TOPK * PAGES_PER_BLOCK
    tiles_per_head = n_new * tiles_per_tok
    b = pl.program_id(0)
    slot = b % 2

    def tile_copies(hh, phys, sl, j):
        rows = pl.ds(pl.multiple_of(hh * HEAD_DIM, HEAD_DIM), HEAD_DIM)
        return (pltpu.make_async_copy(ck_ref.at[layer, phys, rows, :], kbuf_ref.at[sl, j], sem_ref.at[sl, 0]),
                pltpu.make_async_copy(cv_ref.at[layer, phys, rows, :], vbuf_ref.at[sl, j], sem_ref.at[sl, 1]))

    def issue(bb, sl):
        def per_head(hh, carry):
            base = (bb * N_HEADS + hh) * (n_new * MOBA_TOPK)
            for t in range(n_new):
                for r in range(MOBA_TOPK):
                    blk = idx_ref[base + t * MOBA_TOPK + r]
                    for pg in range(PAGES_PER_BLOCK):
                        logical = jnp.minimum(blk * PAGES_PER_BLOCK + pg, n_pages - 1)
                        j = hh * tiles_per_head + (t * MOBA_TOPK + r) * PAGES_PER_BLOCK + pg
                        ck, cv = tile_copies(hh, pt_ref[bb, logical], sl, j)
                        ck.start()
                        cv.start()
            return carry

        lax.fori_loop(0, N_HEADS, per_head, 0)

    @pl.when(b == 0)
    def _():
        issue(b, slot)
        o_ref[...] = jnp.zeros(o_ref.shape, F32)

    @pl.when(b + 1 < pl.num_programs(0))
    def _():
        issue(b + 1, 1 - slot)

    for j in range(N_HEADS * tiles_per_head):
        ck, cv = tile_copies(0, 0, slot, j)
        ck.wait()
        cv.wait()

    lane = lax.broadcasted_iota(jnp.int32, (1, qT_ref.shape[1]), 1)
    for hh in range(N_HEADS):
        rows = slice(hh * HEAD_DIM, (hh + 1) * HEAD_DIM)
        qT = qT_ref[rows, :]
        k_new = kT_ref[rows, :]
        v_new = vT_ref[rows, :]
        out = o_ref[rows, :]
        for t in range(n_new):
            col = b * n_new + t
            first = hh * tiles_per_head + t * tiles_per_tok
            qcol = jnp.sum(jnp.where(lane == col, qT, 0.0), axis=1, keepdims=True) * SM_SCALE
            scores = [jnp.sum(kbuf_ref[slot, first + j] * qcol, axis=0, keepdims=True) for j in range(tiles_per_tok)]
            own_ok = (lane >= b * n_new) & (lane <= col)
            scores.append(jnp.where(own_ok, jnp.sum(k_new * qcol, axis=0, keepdims=True), NEG_INF))
            m = scores[0]
            for s in scores[1:]:
                m = jnp.maximum(m, s)
            m = jnp.max(m, axis=1, keepdims=True)
            probs = [jnp.exp(s - m) for s in scores]
            tot = probs[0]
            for p in probs[1:]:
                tot = tot + p
            denom = jnp.sum(tot, axis=1, keepdims=True)
            pv = v_new * probs[-1]
            for j in range(tiles_per_tok):
                pv = pv + vbuf_ref[slot, first + j] * probs[j]
            o_col = jnp.sum(pv, axis=1, keepdims=True) / denom
            out = jnp.where(lane == col, o_col, out)
        o_ref[rows, :] = out


def _attend_sample(idx_flat, page_table, qT, kT, vT, cacheT_k, cacheT_v, layer, n_new):
    n_seq = page_table.shape[0]
    n_tok = qT.shape[1]
    n_tiles = N_HEADS * n_new * MOBA_TOPK * PAGES_PER_BLOCK
    whole = pl.BlockSpec((ATTN_WIDTH, n_tok), lambda b, idx, pt: (0, 0))
    hbm = pl.BlockSpec(memory_space=pl.ANY)
    return pl.pallas_call(
        functools.partial(_attend_kernel, layer=layer, n_new=n_new),
        grid_spec=pltpu.PrefetchScalarGridSpec(
            num_scalar_prefetch=2,
            grid=(n_seq,),
            in_specs=[whole, whole, whole, hbm, hbm],
            out_specs=whole,
            scratch_shapes=[pltpu.VMEM((2, n_tiles, HEAD_DIM, PAGE_SIZE), F32),
                            pltpu.VMEM((2, n_tiles, HEAD_DIM, PAGE_SIZE), F32),
                            pltpu.SemaphoreType.DMA((2, 2))]),
        out_shape=jax.ShapeDtypeStruct((ATTN_WIDTH, n_tok), F32),
        compiler_params=_cparams("arbitrary"),
        name="attend_sample",
    )(idx_flat, page_table, qT, kT, vT, cacheT_k, cacheT_v)


def kernel(x_prompt, x_sample, cache_k, cache_v, state_pool, page_table, c_prompt, c_sample, w_ada, b_ada, g_mix_pre, g_mix_post, w_in, pool_lin, pool_scale, w_branch_pool, w_branch_attn, w_out, g_ffn_pre, g_ffn_post, w_ff1, w_ff2):
    depth = w_ada.shape[0]
    bp, sp, _ = x_prompt.shape
    bs, ts, _ = x_sample.shape
    n_pages = page_table.shape[1]
    n_blocks = n_pages // PAGES_PER_BLOCK
    n_phys = cache_k.shape[1]

    cacheT_k = jnp.transpose(cache_k, (0, 1, 3, 4, 2)).reshape(depth, n_phys, ATTN_WIDTH, PAGE_SIZE)
    cacheT_v = jnp.transpose(cache_v, (0, 1, 3, 4, 2)).reshape(depth, n_phys, ATTN_WIDTH, PAGE_SIZE)

    ada = _ada(jnp.concatenate([c_prompt, c_sample], axis=0), w_ada, b_ada)

    head_mask = (jnp.arange(ATTN_WIDTH)[None, :] // HEAD_DIM == jnp.arange(N_HEADS)[:, None]).astype(F32)

    xp = x_prompt.reshape(bp * sp, D_MODEL)
    xs = x_sample.reshape(bs * ts, D_MODEL)
    kp_l, vp_l, hp_l, ks_l, vs_l, hs_l = [], [], [], [], [], []
    for l in range(depth):
        wu = w_in[l, :, :POOL_WIDTH].astype(BF16)
        wqkvT = w_in[l, :, POOL_WIDTH:POOL_WIDTH + 3 * ATTN_WIDTH].T.astype(BF16)
        wg = w_in[l, :, POOL_WIDTH + 3 * ATTN_WIDTH:].astype(BF16)
        lin = pool_lin[l].astype(BF16)
        ps = pool_scale[l].reshape(1, POOL_WIDTH)
        wbp = w_branch_pool[l].astype(BF16)
        wba = w_branch_attn[l].astype(BF16)
        wo = w_out[l].astype(BF16)
        w1 = w_ff1[l].astype(BF16)
        w2 = w_ff2[l].astype(BF16)
        gpre = g_mix_pre[l].reshape(1, D_MODEL)
        gpost = g_mix_post[l].reshape(1, D_MODEL)
        gfpre = g_ffn_pre[l].reshape(1, D_MODEL)
        gfpost = g_ffn_post[l].reshape(1, D_MODEL)
        gains = (gpre, gpost, gfpre, gfpost)
        weights = (wg, lin, ps, wbp, wba, wo, w1, w2)
        mod_p = ada[l, :bp].reshape(bp, 1, ADA_CHUNKS * D_MODEL)
        mod_s = jnp.repeat(ada[l, bp:], ts, axis=0).reshape(1, bs * ts, ADA_CHUNKS * D_MODEL)

        u, qT, kT, vT = _inproj(xp, mod_p, gpre, wu, wqkvT, tm=TOKEN_TILE, seq=sp)
        attn = _moba_prompt(qT, kT, vT).reshape(bp * sp, ATTN_WIDTH)
        kp_l.append(kT.reshape(bp * ATTN_WIDTH, sp))
        vp_l.append(vT.reshape(bp * ATTN_WIDTH, sp))
        if l == 0:
            xp, kmeanT = _tail(xp, u, attn, mod_p, gains, weights, tm=STREAM_TILE, seq=sp, pool_in_kernel=True,
                               cache=(page_table, cacheT_k))
        elif l == depth - 1:
            xp, k_all, v_all = _tail(xp, u, attn, mod_p, gains, weights, tm=TOKEN_TILE, seq=sp, pool_in_kernel=True,
                                     stack=(kp_l, vp_l))
        else:
            xp, = _tail(xp, u, attn, mod_p, gains, weights, tm=TOKEN_TILE, seq=sp, pool_in_kernel=True)
        hp_l.append(u.reshape(bp, sp, POOL_WIDTH)[:, sp - POOL_HIST:, :])

        n_s = bs * ts
        u_s, qT_s, kT_s, vT_s = _inproj(xs, mod_s, gpre, wu, wqkvT, tm=n_s, seq=n_s)
        qT_s, kT_s, vT_s = qT_s[0], kT_s[0], vT_s[0]
        z = jnp.concatenate([state_pool[l], u_s.reshape(bs, ts, POOL_WIDTH)], axis=1)
        d_s = _pool_sample(jnp.transpose(z, (1, 0, 2)), ts)
        d_s = jnp.transpose(d_s, (1, 0, 2)).reshape(n_s, POOL_WIDTH)
        q_s = qT_s.T.reshape(bs, 1, ts, ATTN_WIDTH)
        q_exp = (q_s * head_mask[None, :, None, :]).reshape(bs, N_HEADS * ts, ATTN_WIDTH)
        idx = _select_sample(q_exp, kmeanT, l, n_blocks)[:, :, :MOBA_TOPK]
        attnT_s = _attend_sample(idx.reshape(-1), page_table, qT_s, kT_s, vT_s, cacheT_k, cacheT_v, l, ts)
        attn_s = attnT_s.T.astype(BF16)
        xs, = _tail(xs, d_s, attn_s, mod_s, gains, weights, tm=n_s, seq=n_s, pool_in_kernel=False)
        ks_l.append(kT_s.T.reshape(bs, ts, N_HEADS, HEAD_DIM))
        vs_l.append(vT_s.T.reshape(bs, ts, N_HEADS, HEAD_DIM))
        hs_l.append(z[:, ts:, :])

    def untranspose(stacked):
        return jnp.transpose(stacked.reshape(depth, bp, N_HEADS, HEAD_DIM, sp), (0, 1, 4, 2, 3))

    return (xp.reshape(bp, sp, D_MODEL), xs.reshape(bs, ts, D_MODEL),
            untranspose(k_all), untranspose(v_all), jnp.stack(hp_l),
            jnp.stack(ks_l), jnp.stack(vs_l), jnp.stack(hs_l))
```

```python
import functools

import jax
import jax.numpy as jnp
from jax import lax
from jax.experimental import pallas as pl
from jax.experimental.pallas import tpu as pltpu

F32 = jnp.float32
BF16 = jnp.bfloat16

D_MODEL = 1024
N_HEADS = 8
HEAD_DIM = 64
ATTN_WIDTH = N_HEADS * HEAD_DIM
MOBA_BLOCK = 256
MOBA_TOPK = 3
PAGE_SIZE = 128
PAGES_PER_BLOCK = MOBA_BLOCK // PAGE_SIZE
POOL_WINDOWS = (2, 4, 8, 16)
POOL_WIDTH = 512
POOL_GROUP_WIDTH = 128
POOL_HIST = 15
HIST_PAD = 16
D_FF = 4 * D_MODEL
ADA_CHUNKS = 6
EPS = 1e-6
NEG_INF = -1e30
SM_SCALE = HEAD_DIM ** -0.5
LOG2_E = 1.4426950408889634
V_ROWS = HEAD_DIM + 16

V7X_VMEM_LIMIT = 56 * 1024 * 1024

TOKEN_TILE = 512
STREAM_TILE = 256
SUB_TILE = 256
STREAM_CHUNKS = 4


def _cparams(*sem):
    return pltpu.CompilerParams(dimension_semantics=sem, vmem_limit_bytes=V7X_VMEM_LIMIT)


def _rms(x, g):
    return x * lax.rsqrt(jnp.mean(x * x, axis=-1, keepdims=True) + EPS) * g


def _sigmoid(x):
    return 1.0 / (1.0 + jnp.exp(-x))


def _bdot(a, b):
    return jnp.dot(a, b, preferred_element_type=F32)


def _ada_kernel(c_ref, w_ref, b_ref, o_ref):
    o_ref[0] = _bdot(c_ref[...].astype(BF16), w_ref[0].astype(BF16)) + b_ref[0]


def _ada(c_all, w_ada, b_ada):
    depth, _, width = w_ada.shape
    n = c_all.shape[0]
    tn = 1536
    return pl.pallas_call(
        _ada_kernel,
        grid=(depth, width // tn),
        in_specs=[pl.BlockSpec((n, D_MODEL), lambda l, j: (0, 0)),
                  pl.BlockSpec((1, D_MODEL, tn), lambda l, j: (l, 0, j)),
                  pl.BlockSpec((1, 1, tn), lambda l, j: (l, 0, j))],
        out_specs=pl.BlockSpec((1, n, tn), lambda l, j: (l, 0, j)),
        out_shape=jax.ShapeDtypeStruct((depth, n, width), F32),
        compiler_params=_cparams("arbitrary", "arbitrary"),
        name="ada",
    )(c_all, w_ada, b_ada.reshape(depth, 1, width))


def _mod_spec(mod, chunk, tiles_per_group):
    rows = mod.shape[1]
    return pl.BlockSpec((1, rows, D_MODEL), lambda i: (i // tiles_per_group, 0, chunk))


def _modulated(x_ref, g_ref, sc_ref, sh_ref):
    return (_rms(x_ref[...], g_ref[...]) * (1.0 + sc_ref[0]) + sh_ref[0]).astype(BF16)


def _inproj_kernel(x_ref, sc_ref, sh_ref, g_ref, wu_ref, wqkv_ref, u_ref, qT_ref, kT_ref, vT_ref):
    h = _modulated(x_ref, g_ref, sc_ref, sh_ref)
    u_ref[...] = _bdot(h, wu_ref[...])
    qkvT = lax.dot_general(wqkv_ref[...], h, (((1,), (1,)), ((), ())), preferred_element_type=F32)
    qT_ref[0] = qkvT[0:ATTN_WIDTH]
    kT_ref[0] = qkvT[ATTN_WIDTH:2 * ATTN_WIDTH]
    vT_ref[0] = qkvT[2 * ATTN_WIDTH:3 * ATTN_WIDTH]


def _inproj(x, mod, g, wu, wqkvT, *, tm, seq):
    n_tok = x.shape[0]
    tpg = seq // tm
    n_seq = n_tok // seq
    const = lambda i: (0, 0)
    tok = lambda w: pl.BlockSpec((tm, w), lambda i: (i, 0))
    tspec = pl.BlockSpec((1, ATTN_WIDTH, tm), lambda i: (i // tpg, 0, i % tpg))
    tshape = jax.ShapeDtypeStruct((n_seq, ATTN_WIDTH, seq), F32)
    return pl.pallas_call(
        _inproj_kernel,
        grid=(n_tok // tm,),
        in_specs=[tok(D_MODEL), _mod_spec(mod, 1, tpg), _mod_spec(mod, 0, tpg),
                  pl.BlockSpec((1, D_MODEL), const),
                  pl.BlockSpec(wu.shape, const), pl.BlockSpec(wqkvT.shape, const)],
        out_specs=[tok(POOL_WIDTH), tspec, tspec, tspec],
        out_shape=[jax.ShapeDtypeStruct((n_tok, POOL_WIDTH), F32), tshape, tshape, tshape],
        compiler_params=_cparams("arbitrary"),
        name="inproj",
    )(x, mod, mod, g, wu, wqkvT)


def _pool_stage(u_ref, z_ref, tile_in_seq):
    tm = u_ref.shape[0]

    @pl.when(tile_in_seq == 0)
    def _():
        z_ref[0:HIST_PAD, :] = jnp.zeros((HIST_PAD, POOL_WIDTH), F32)

    @pl.when(tile_in_seq > 0)
    def _():
        z_ref[0:HIST_PAD, :] = z_ref[tm:tm + HIST_PAD, :]

    z_ref[HIST_PAD:HIST_PAD + tm, :] = u_ref[...]


def _pool_rows(z_ref, row0, n_rows, pos0):
    pos = pos0 + lax.broadcasted_iota(jnp.int32, (n_rows, POOL_GROUP_WIDTH), 0)
    parts = []
    for g, w in enumerate(POOL_WINDOWS):
        cols = slice(g * POOL_GROUP_WIDTH, (g + 1) * POOL_GROUP_WIDTH)
        base = HIST_PAD + row0
        cur = z_ref[base:base + n_rows, cols]
        acc = cur
        for r in range(1, w):
            acc = acc + z_ref[base - r:base - r + n_rows, cols]
        cnt = jnp.minimum(pos + 1, w).astype(F32)
        parts.append((acc / cnt - cur).astype(BF16))
    return parts


def _pool_sample_kernel(z_ref, d_ref):
    n_new = d_ref.shape[0]
    for t in range(n_new):
        for g, w in enumerate(POOL_WINDOWS):
            cols = slice(g * POOL_GROUP_WIDTH, (g + 1) * POOL_GROUP_WIDTH)
            cur = z_ref[POOL_HIST + t, :, cols]
            acc = cur
            for r in range(1, w):
                acc = acc + z_ref[POOL_HIST + t - r, :, cols]
            d_ref[t, :, cols] = (acc / float(w) - cur).astype(BF16)


def _pool_sample(z_tm, n_new):
    _, bd, _ = z_tm.shape
    return pl.pallas_call(
        _pool_sample_kernel,
        out_shape=jax.ShapeDtypeStruct((n_new, bd, POOL_WIDTH), BF16),
        name="pool_sample",
    )(z_tm)


def _moba_kernel(qT_ref, kT_ref, vT_ref, o_ref, kb_ref, vb_ref, qm_ref, sel_ref, st_ref, p_ref):
    hp = 2 * HEAD_DIM
    s_len = qT_ref.shape[2]
    nb = s_len // MOBA_BLOCK
    qT = qT_ref[0]
    k = kT_ref[0].T
    kmean = jnp.sum(k.reshape(nb, MOBA_BLOCK, hp), axis=1) * (1.0 / MOBA_BLOCK)
    kb_ref[...] = k.astype(BF16)
    ones = jnp.ones((V_ROWS - HEAD_DIM, s_len), BF16)
    for hh in range(2):
        vb_ref[hh, 0:HEAD_DIM, :] = vT_ref[0, hh * HEAD_DIM:(hh + 1) * HEAD_DIM, :].astype(BF16)
        vb_ref[hh, HEAD_DIM:V_ROWS, :] = ones

    row_head = lax.broadcasted_iota(jnp.int32, (hp, s_len), 0) // HEAD_DIM
    lane_head = lax.broadcasted_iota(jnp.int32, (nb, hp), 1) // HEAD_DIM
    blk = lax.broadcasted_iota(jnp.int32, (nb, s_len), 0)
    n_past = lax.broadcasted_iota(jnp.int32, (nb, s_len), 1) // MOBA_BLOCK
    valid = blk < n_past
    for hh in range(2):
        qm = jnp.where(row_head == hh, qT, 0.0)
        for i in range(nb):
            qm_ref[hh, i] = (qm[:, i * MOBA_BLOCK:(i + 1) * MOBA_BLOCK] * (SM_SCALE * LOG2_E)).astype(BF16)
        sc = jnp.dot(jnp.where(lane_head == hh, kmean, 0.0), qT, preferred_element_type=F32,
                     precision=lax.Precision.HIGHEST)
        sc = jnp.where(valid, sc, NEG_INF)
        rank = jnp.zeros((nb, s_len), F32)
        for m in range(nb):
            row = sc[m:m + 1, :]
            beats = (row > sc) | ((row == sc) & (blk > m))
            rank = rank + jnp.where(beats, 1.0, 0.0)
        sel_ref[hh] = jnp.where(valid & (rank < float(MOBA_TOPK)), 1.0, 0.0)

    key_pos = lax.broadcasted_iota(jnp.int32, (MOBA_BLOCK, MOBA_BLOCK), 0)
    q_pos = lax.broadcasted_iota(jnp.int32, (MOBA_BLOCK, MOBA_BLOCK), 1)
    causal = key_pos <= q_pos
    units = [(i, hh) for i in range(nb) for hh in range(2)]
    state = {}

    def blocks(j):
        return slice(j * MOBA_BLOCK, (j + 1) * MOBA_BLOCK)

    def scores(n):
        i, hh = units[n]
        picked = [sel_ref[hh, j:j + 1, blocks(i)] > 0.0 for j in range(i)]
        st_ref[n % 2, 0:(i + 1) * MOBA_BLOCK, :] = _bdot(kb_ref[0:(i + 1) * MOBA_BLOCK, :], qm_ref[hh, i])
        m_run = jnp.max(jnp.where(causal, st_ref[n % 2, blocks(i), :], NEG_INF), axis=0, keepdims=True)
        for j in range(i):
            cm = jnp.max(st_ref[n % 2, blocks(j), :], axis=0, keepdims=True)
            m_run = jnp.maximum(m_run, jnp.where(picked[j], cm, NEG_INF))
        state[n] = (m_run, picked)

    def values(n):
        i, hh = units[n]
        m_run, picked = state.pop(n)
        for j in range(i + 1):
            st = st_ref[n % 2, blocks(j), :]
            if j == i:
                st = jnp.where(causal, st, NEG_INF)
                shift = m_run
            else:
                shift = jnp.where(picked[j], m_run, -NEG_INF)
            p_ref[n % 2, blocks(j), :] = jnp.exp2(st - shift).astype(BF16)
        acc = _bdot(vb_ref[hh, :, 0:(i + 1) * MOBA_BLOCK], p_ref[n % 2, 0:(i + 1) * MOBA_BLOCK, :])
        state[("out", i, hh)] = acc[0:HEAD_DIM] / acc[HEAD_DIM:HEAD_DIM + 1]
        if hh == 1:
            both = jnp.concatenate([state.pop(("out", i, 0)), state.pop(("out", i, 1))], axis=0)
            o_ref[0, blocks(i), :] = both.T.astype(BF16)

    for n in range(len(units) + 1):
        if n < len(units):
            scores(n)
        if n > 0:
            values(n - 1)


def _moba_prompt(qT, kT, vT):
    b, _, s = qT.shape
    hp = 2 * HEAD_DIM
    nb = s // MOBA_BLOCK
    spec = pl.BlockSpec((1, hp, s), lambda i, p: (i, p, 0))
    return pl.pallas_call(
        _moba_kernel,
        grid=(b, ATTN_WIDTH // hp),
        in_specs=[spec, spec, spec],
        out_specs=pl.BlockSpec((1, s, hp), lambda i, p: (i, 0, p)),
        out_shape=jax.ShapeDtypeStruct((b, s, ATTN_WIDTH), BF16),
        scratch_shapes=[pltpu.VMEM((s, hp), BF16),
                        pltpu.VMEM((2, V_ROWS, s), BF16),
                        pltpu.VMEM((2, nb, hp, MOBA_BLOCK), BF16),
                        pltpu.VMEM((2, nb, s), F32),
                        pltpu.VMEM((2, s, MOBA_BLOCK), F32),
                        pltpu.VMEM((2, s, MOBA_BLOCK), BF16)],
        compiler_params=_cparams("arbitrary", "arbitrary"),
        name="moba_prompt",
    )(qT, kT, vT)


def _cache_stream(pt_ref, ck_ref, buf_ref, sem_ref, *, n_seq, n_pages, pages_per_step):
    cp = pages_per_step // STREAM_CHUNKS

    def page_copy(layer, phys, k, p):
        return pltpu.make_async_copy(ck_ref.at[layer, phys], buf_ref.at[k, p], sem_ref.at[k])

    def issue(st, k):
        first = st * pages_per_step + k * cp
        layer = first // (n_seq * n_pages)
        seq = (first // n_pages) % n_seq
        page0 = first % n_pages
        for p in range(cp):
            page_copy(layer, pt_ref[seq, page0 + p], k, p).start()

    def point(step, k, acc):
        for p in range(cp):
            page_copy(0, 0, k, p).wait()
        if k == 0:
            issue(step, STREAM_CHUNKS - 1)
        else:
            @pl.when(step + 1 < pl.num_programs(0))
            def _():
                issue(step + 1, k - 1)
        lane = lax.broadcasted_iota(jnp.int32, (ATTN_WIDTH, PAGE_SIZE), 1)
        block0 = ((step * pages_per_step + k * cp) % n_pages) // PAGES_PER_BLOCK
        for b2 in range(cp // PAGES_PER_BLOCK):
            t = buf_ref[k, PAGES_PER_BLOCK * b2]
            for e in range(1, PAGES_PER_BLOCK):
                t = t + buf_ref[k, PAGES_PER_BLOCK * b2 + e]
            acc = jnp.where(lane == block0 + b2, jnp.sum(t, axis=1, keepdims=True), acc)
        return acc

    return issue, point


def _tail_kernel(*refs, pool_in_kernel, tiles_per_seq, stream, stack, n_steps):
    (x_ref, p_ref, at_ref, sc1_ref, sh1_ref, gt1_ref, sc2_ref, sh2_ref, gt2_ref,
     gpre_ref, gpost_ref, gfpre_ref, gfpost_ref, wg_ref, lin_ref, ps_ref, wbp_ref, wba_ref, wout_ref,
     w1_ref, w2_ref) = refs[:21]
    rest = list(refs[21:])
    if stream is not None:
        pt_ref, ck_ref = rest[:2]
        rest = rest[2:]
    n_groups, per_group = stack
    n_src = n_groups * per_group
    stack_src, rest = rest[:n_src], rest[n_src:]
    o_ref = rest.pop(0)
    if stream is not None:
        km_ref = rest.pop(0)
    stack_dst, rest = rest[:n_groups], rest[n_groups:]
    z_ref = rest.pop(0) if pool_in_kernel else None
    stack_sem = rest.pop() if n_src else None
    stage_ref = rest.pop() if n_src else None
    step = pl.program_id(0)

    def stack_copies(direction):
        copies = []
        for n, s in enumerate(stack_src):
            g, l = divmod(n, per_group)
            rows_per_step = s.shape[0] // n_steps
            rows = pl.ds(pl.multiple_of(step * rows_per_step, 8), rows_per_step)
            if direction == 0:
                copies.append(pltpu.make_async_copy(s.at[rows, :], stage_ref.at[n], stack_sem.at[0, n]))
            else:
                copies.append(pltpu.make_async_copy(stage_ref.at[n], stack_dst[g].at[l, rows, :], stack_sem.at[1, n]))
        return copies

    for cp in stack_copies(0):
        cp.start()

    def stream_point(k):
        pass

    if stream is not None:
        buf_ref, sem_ref, kacc_ref = rest
        issue, point = _cache_stream(pt_ref, ck_ref, buf_ref, sem_ref, **stream)

        @pl.when(step == 0)
        def _():
            for k in range(STREAM_CHUNKS - 1):
                issue(step, k)

        @pl.when((step * stream["pages_per_step"]) % stream["n_pages"] == 0)
        def _():
            kacc_ref[...] = jnp.zeros(kacc_ref.shape, F32)

        def stream_point(k):
            acc = point(step, k, kacc_ref[...])
            kacc_ref[...] = acc
            if k == STREAM_CHUNKS - 1:
                km_ref[0, 0] = acc * (1.0 / MOBA_BLOCK)

    tm = x_ref.shape[0]
    sub = min(tm, SUB_TILE)
    tile_in_seq = step % tiles_per_seq
    if pool_in_kernel:
        _pool_stage(p_ref, z_ref, tile_in_seq)

    def mod_rows(ref, rows):
        return ref[0] if ref.shape[1] == 1 else ref[0, rows, :]

    for s in range(tm // sub):
        rows = slice(s * sub, (s + 1) * sub)
        here = stream_point if s == 0 else (lambda k: None)
        here(0)
        x = x_ref[rows, :]
        h = (_rms(x, gpre_ref[...]) * (1.0 + mod_rows(sc1_ref, rows)) + mod_rows(sh1_ref, rows)).astype(BF16)
        gate_pool = _sigmoid(_bdot(h, wg_ref[:, 0:D_MODEL]))
        gate_attn = _sigmoid(_bdot(h, wg_ref[:, D_MODEL:2 * D_MODEL]))
        if pool_in_kernel:
            d_parts = _pool_rows(z_ref, s * sub, sub, tile_in_seq * tm + s * sub)
        else:
            d_parts = [p_ref[rows, g * POOL_GROUP_WIDTH:(g + 1) * POOL_GROUP_WIDTH] for g in range(len(POOL_WINDOWS))]
        mixed = jnp.concatenate([_bdot(d_parts[g], lin_ref[g]) for g in range(len(POOL_WINDOWS))],
                                axis=-1) * ps_ref[...]
        merged = (gate_pool * _bdot(mixed.astype(BF16), wbp_ref[...])
                  + gate_attn * _bdot(at_ref[rows, :], wba_ref[...]))
        here(1)
        x1 = x + mod_rows(gt1_ref, rows) * _rms(_bdot(merged.astype(BF16), wout_ref[...]), gpost_ref[...])
        h2 = (_rms(x1, gfpre_ref[...]) * (1.0 + mod_rows(sc2_ref, rows)) + mod_rows(sh2_ref, rows)).astype(BF16)
        here(2)
        if s == 0:
            for cp in stack_copies(0):
                cp.wait()
            for cp in stack_copies(1):
                cp.start()
        ff = jnp.zeros(x.shape, F32)
        for c in range(D_FF // D_MODEL):
            cols = slice(c * D_MODEL, (c + 1) * D_MODEL)
            a = jnp.maximum(_bdot(h2, w1_ref[:, cols]), 0.0)
            ff = ff + _bdot((a * a).astype(BF16), w2_ref[cols, :])
            if c == 1:
                here(3)
        o_ref[rows, :] = x1 + mod_rows(gt2_ref, rows) * _rms(ff, gfpost_ref[...])

    for cp in stack_copies(1):
        cp.wait()


def _tail(x, pool_in, attn, mod, gains, weights, *, tm, seq, pool_in_kernel, cache=None, stack=()):
    n_tok = x.shape[0]
    n_steps = n_tok // tm
    tpg = seq // tm
    tok = lambda w: pl.BlockSpec((tm, w), lambda i: (i, 0))
    full = lambda a: pl.BlockSpec(a.shape, lambda i: (0,) * a.ndim, pipeline_mode=pl.Buffered(1))
    mods = [_mod_spec(mod, c, tpg) for c in (1, 0, 2, 4, 3, 5)]
    in_specs = ([tok(D_MODEL), tok(POOL_WIDTH), tok(ATTN_WIDTH)] + mods
                + [full(a) for a in gains] + [full(a) for a in weights])
    operands = [x, pool_in, attn, *([mod] * 6), *gains, *weights]
    out_specs = [tok(D_MODEL)]
    out_shape = [jax.ShapeDtypeStruct((n_tok, D_MODEL), F32)]
    scratch = [pltpu.VMEM((HIST_PAD + tm, POOL_WIDTH), F32)] if pool_in_kernel else []
    stream = None
    if cache is not None:
        page_table, cacheT_k = cache
        depth = cacheT_k.shape[0]
        n_seq, n_pages = page_table.shape
        pages_per_step, rem = divmod(depth * n_seq * n_pages, n_steps)
        assert rem == 0 and n_pages % pages_per_step == 0 and pages_per_step % (STREAM_CHUNKS * PAGES_PER_BLOCK) == 0
        assert n_pages // PAGES_PER_BLOCK <= PAGE_SIZE
        stream = dict(n_seq=n_seq, n_pages=n_pages, pages_per_step=pages_per_step)
        in_specs += [pl.BlockSpec(memory_space=pltpu.SMEM), pl.BlockSpec(memory_space=pl.ANY)]
        operands += [page_table, cacheT_k]
        unit = lambda i: (i * pages_per_step) // n_pages
        out_specs.append(pl.BlockSpec((1, 1, ATTN_WIDTH, PAGE_SIZE), lambda i: (unit(i) // n_seq, unit(i) % n_seq, 0, 0)))
        out_shape.append(jax.ShapeDtypeStruct((depth, n_seq, ATTN_WIDTH, PAGE_SIZE), F32))
        scratch += [pltpu.VMEM((STREAM_CHUNKS, pages_per_step // STREAM_CHUNKS, ATTN_WIDTH, PAGE_SIZE), F32),
                    pltpu.SemaphoreType.DMA((STREAM_CHUNKS,)),
                    pltpu.VMEM((ATTN_WIDTH, PAGE_SIZE), F32)]
    per_group = len(stack[0]) if stack else 0
    if stack:
        hbm = pl.BlockSpec(memory_space=pl.ANY)
        flat = [a for group in stack for a in group]
        assert all(len(g) == per_group for g in stack) and all(a.shape[0] % (8 * n_steps) == 0 for a in flat)
        in_specs += [hbm] * len(flat)
        operands += flat
        out_specs += [hbm] * len(stack)
        out_shape += [jax.ShapeDtypeStruct((per_group,) + g[0].shape, g[0].dtype) for g in stack]
        scratch += [pltpu.VMEM((len(flat), flat[0].shape[0] // n_steps, flat[0].shape[1]), flat[0].dtype),
                    pltpu.SemaphoreType.DMA((2, len(flat)))]
    outs = pl.pallas_call(
        functools.partial(_tail_kernel, pool_in_kernel=pool_in_kernel, tiles_per_seq=tpg, stream=stream,
                          stack=(len(stack), per_group), n_steps=n_steps),
        grid=(n_steps,),
        in_specs=in_specs,
        out_specs=out_specs,
        out_shape=out_shape,
        scratch_shapes=scratch,
        compiler_params=_cparams("arbitrary"),
        name="tail",
    )(*operands)
    return outs


def _select_kernel(q_ref, km_ref, o_ref, *, n_blocks):
    sc = jnp.dot(q_ref[0], km_ref[0, 0], preferred_element_type=F32, precision=lax.Precision.HIGHEST)
    lane = lax.broadcasted_iota(jnp.int32, sc.shape, 1)
    cur = jnp.where(lane < n_blocks, sc, -jnp.inf)
    out = jnp.zeros(sc.shape, jnp.int32)
    for r in range(MOBA_TOPK):
        m = jnp.max(cur, axis=1, keepdims=True)
        pick = jnp.min(jnp.where(cur == m, lane, PAGE_SIZE), axis=1, keepdims=True)
        out = jnp.where(lane == r, pick, out)
        cur = jnp.where(lane == pick, -jnp.inf, cur)
    o_ref[0] = out


def _select_sample(q_exp, kmeanT, layer, n_blocks):
    n_seq, rows, _ = q_exp.shape
    return pl.pallas_call(
        functools.partial(_select_kernel, n_blocks=n_blocks),
        grid=(n_seq,),
        in_specs=[pl.BlockSpec((1, rows, ATTN_WIDTH), lambda b: (b, 0, 0)),
                  pl.BlockSpec((1, 1, ATTN_WIDTH, PAGE_SIZE), lambda b: (layer, b, 0, 0))],
        out_specs=pl.BlockSpec((1, rows, PAGE_SIZE), lambda b: (b, 0, 0)),
        out_shape=jax.ShapeDtypeStruct((n_seq, rows, PAGE_SIZE), jnp.int32),
        compiler_params=_cparams("arbitrary"),
        name="select_sample",
    )(q_exp, kmeanT)


def _attend_kernel(idx_ref, pt_ref, qT_ref, kT_ref, vT_ref, ck_ref, cv_ref, o_ref, kbuf_ref, vbuf_ref, sem_ref,
                   *, layer, n_new):
    n_seq, n_pages = pt_ref.shape
    tiles_per_tok = MOBA_TOPK * PAGES_PER_BLOCK
    tiles_per_head = n_new * tiles_per_tok
    b = pl.program_id(0)
    slot = b % 2

    def tile_copies(hh, phys, sl, j):
        rows = pl.ds(pl.multiple_of(hh * HEAD_DIM, HEAD_DIM), HEAD_DIM)
        return (pltpu.make_async_copy(ck_ref.at[layer, phys, rows, :], kbuf_ref.at[sl, j], sem_ref.at[sl, 0]),
                pltpu.make_async_copy(cv_ref.at[layer, phys, rows, :], vbuf_ref.at[sl, j], sem_ref.at[sl, 1]))

    def issue(bb, sl):
        def per_head(hh, carry):
            base = (bb * N_HEADS + hh) * (n_new * MOBA_TOPK)
            for t in range(n_new):
                for r in range(MOBA_TOPK):
                    blk = idx_ref[base + t * MOBA_TOPK + r]
                    for pg in range(PAGES_PER_BLOCK):
                        logical = jnp.minimum(blk * PAGES_PER_BLOCK + pg, n_pages - 1)
                        j = hh * tiles_per_head + (t * MOBA_TOPK + r) * PAGES_PER_BLOCK + pg
                        ck, cv = tile_copies(hh, pt_ref[bb, logical], sl, j)
                        ck.start()
                        cv.start()
            return carry

        lax.fori_loop(0, N_HEADS, per_head, 0)

    @pl.when(b == 0)
    def _():
        issue(b, slot)
        o_ref[...] = jnp.zeros(o_ref.shape, F32)

    @pl.when(b + 1 < pl.num_programs(0))
    def _():
        issue(b + 1, 1 - slot)

    for j in range(N_HEADS * tiles_per_head):
        ck, cv = tile_copies(0, 0, slot, j)
        ck.wait()
        cv.wait()

    lane = lax.broadcasted_iota(jnp.int32, (1, qT_ref.shape[1]), 1)
    for hh in range(N_HEADS):
        rows = slice(hh * HEAD_DIM, (hh + 1) * HEAD_DIM)
        qT = qT_ref[rows, :]
        k_new = kT_ref[rows, :]
        v_new = vT_ref[rows, :]
        out = o_ref[rows, :]
        for t in range(n_new):
            col = b * n_new + t
            first = hh * tiles_per_head + t * tiles_per_tok
            qcol = jnp.sum(jnp.where(lane == col, qT, 0.0), axis=1, keepdims=True) * SM_SCALE
            scores = [jnp.sum(kbuf_ref[slot, first + j] * qcol, axis=0, keepdims=True) for j in range(tiles_per_tok)]
            own_ok = (lane >= b * n_new) & (lane <= col)
            scores.append(jnp.where(own_ok, jnp.sum(k_new * qcol, axis=0, keepdims=True), NEG_INF))
            m = scores[0]
            for s in scores[1:]:
                m = jnp.maximum(m, s)
            m = jnp.max(m, axis=1, keepdims=True)
            probs = [jnp.exp(s - m) for s in scores]
            tot = probs[0]
            for p in probs[1:]:
                tot = tot + p
            denom = jnp.sum(tot, axis=1, keepdims=True)
            pv = v_new * probs[-1]
            for j in range(tiles_per_tok):
                pv = pv + vbuf_ref[slot, first + j] * probs[j]
            o_col = jnp.sum(pv, axis=1, keepdims=True) / denom
            out = jnp.where(lane == col, o_col, out)
        o_ref[rows, :] = out


def _attend_sample(idx_flat, page_table, qT, kT, vT, cacheT_k, cacheT_v, layer, n_new):
    n_seq = page_table.shape[0]
    n_tok = qT.shape[1]
    n_tiles = N_HEADS * n_new * MOBA_TOPK * PAGES_PER_BLOCK
    whole = pl.BlockSpec((ATTN_WIDTH, n_tok), lambda b, idx, pt: (0, 0))
    hbm = pl.BlockSpec(memory_space=pl.ANY)
    return pl.pallas_call(
        functools.partial(_attend_kernel, layer=layer, n_new=n_new),
        grid_spec=pltpu.PrefetchScalarGridSpec(
            num_scalar_prefetch=2,
            grid=(n_seq,),
            in_specs=[whole, whole, whole, hbm, hbm],
            out_specs=whole,
            scratch_shapes=[pltpu.VMEM((2, n_tiles, HEAD_DIM, PAGE_SIZE), F32),
                            pltpu.VMEM((2, n_tiles, HEAD_DIM, PAGE_SIZE), F32),
                            pltpu.SemaphoreType.DMA((2, 2))]),
        out_shape=jax.ShapeDtypeStruct((ATTN_WIDTH, n_tok), F32),
        compiler_params=_cparams("arbitrary"),
        name="attend_sample",
    )(idx_flat, page_table, qT, kT, vT, cacheT_k, cacheT_v)


def kernel(x_prompt, x_sample, cache_k, cache_v, state_pool, page_table, c_prompt, c_sample, w_ada, b_ada, g_mix_pre, g_mix_post, w_in, pool_lin, pool_scale, w_branch_pool, w_branch_attn, w_out, g_ffn_pre, g_ffn_post, w_ff1, w_ff2):
    depth = w_ada.shape[0]
    bp, sp, _ = x_prompt.shape
    bs, ts, _ = x_sample.shape
    n_pages = page_table.shape[1]
    n_blocks = n_pages // PAGES_PER_BLOCK
    n_phys = cache_k.shape[1]

    cacheT_k = jnp.transpose(cache_k, (0, 1, 3, 4, 2)).reshape(depth, n_phys, ATTN_WIDTH, PAGE_SIZE)
    cacheT_v = jnp.transpose(cache_v, (0, 1, 3, 4, 2)).reshape(depth, n_phys, ATTN_WIDTH, PAGE_SIZE)

    ada = _ada(jnp.concatenate([c_prompt, c_sample], axis=0), w_ada, b_ada)

    head_mask = (jnp.arange(ATTN_WIDTH)[None, :] // HEAD_DIM == jnp.arange(N_HEADS)[:, None]).astype(F32)

    xp = x_prompt.reshape(bp * sp, D_MODEL)
    xs = x_sample.reshape(bs * ts, D_MODEL)
    kp_l, vp_l, hp_l, ks_l, vs_l, hs_l = [], [], [], [], [], []
    for l in range(depth):
        wu = w_in[l, :, :POOL_WIDTH].astype(BF16)
        wqkvT = w_in[l, :, POOL_WIDTH:POOL_WIDTH + 3 * ATTN_WIDTH].T.astype(BF16)
        wg = w_in[l, :, POOL_WIDTH + 3 * ATTN_WIDTH:].astype(BF16)
        lin = pool_lin[l].astype(BF16)
        ps = pool_scale[l].reshape(1, POOL_WIDTH)
        wbp = w_branch_pool[l].astype(BF16)
        wba = w_branch_attn[l].astype(BF16)
        wo = w_out[l].astype(BF16)
        w1 = w_ff1[l].astype(BF16)
        w2 = w_ff2[l].astype(BF16)
        gpre = g_mix_pre[l].reshape(1, D_MODEL)
        gpost = g_mix_post[l].reshape(1, D_MODEL)
        gfpre = g_ffn_pre[l].reshape(1, D_MODEL)
        gfpost = g_ffn_post[l].reshape(1, D_MODEL)
        gains = (gpre, gpost, gfpre, gfpost)
        weights = (wg, lin, ps, wbp, wba, wo, w1, w2)
        mod_p = ada[l, :bp].reshape(bp, 1, ADA_CHUNKS * D_MODEL)
        mod_s = jnp.repeat(ada[l, bp:], ts, axis=0).reshape(1, bs * ts, ADA_CHUNKS * D_MODEL)

        u, qT, kT, vT = _inproj(xp, mod_p, gpre, wu, wqkvT, tm=TOKEN_TILE, seq=sp)
        attn = _moba_prompt(qT, kT, vT).reshape(bp * sp, ATTN_WIDTH)
        kp_l.append(kT.reshape(bp * ATTN_WIDTH, sp))
        vp_l.append(vT.reshape(bp * ATTN_WIDTH, sp))
        if l == 0:
            xp, kmeanT = _tail(xp, u, attn, mod_p, gains, weights, tm=STREAM_TILE, seq=sp, pool_in_kernel=True,
                               cache=(page_table, cacheT_k))
        elif l == depth - 1:
            xp, k_all, v_all = _tail(xp, u, attn, mod_p, gains, weights, tm=TOKEN_TILE, seq=sp, pool_in_kernel=True,
                                     stack=(kp_l, vp_l))
        else:
            xp, = _tail(xp, u, attn, mod_p, gains, weights, tm=TOKEN_TILE, seq=sp, pool_in_kernel=True)
        hp_l.append(u.reshape(bp, sp, POOL_WIDTH)[:, sp - POOL_HIST:, :])

        n_s = bs * ts
        u_s, qT_s, kT_s, vT_s = _inproj(xs, mod_s, gpre, wu, wqkvT, tm=n_s, seq=n_s)
        qT_s, kT_s, vT_s = qT_s[0], kT_s[0], vT_s[0]
        z = jnp.concatenate([state_pool[l], u_s.reshape(bs, ts, POOL_WIDTH)], axis=1)
        d_s = _pool_sample(jnp.transpose(z, (1, 0, 2)), ts)
        d_s = jnp.transpose(d_s, (1, 0, 2)).reshape(n_s, POOL_WIDTH)
        q_s = qT_s.T.reshape(bs, 1, ts, ATTN_WIDTH)
        q_exp = (q_s * head_mask[None, :, None, :]).reshape(bs, N_HEADS * ts, ATTN_WIDTH)
        idx = _select_sample(q_exp, kmeanT, l, n_blocks)[:, :, :MOBA_TOPK]
        attnT_s = _attend_sample(idx.reshape(-1), page_table, qT_s, kT_s, vT_s, cacheT_k, cacheT_v, l, ts)
        attn_s = attnT_s.T.astype(BF16)
        xs, = _tail(xs, d_s, attn_s, mod_s, gains, weights, tm=n_s, seq=n_s, pool_in_kernel=False)
        ks_l.append(kT_s.T.reshape(bs, ts, N_HEADS, HEAD_DIM))
        vs_l.append(vT_s.T.reshape(bs, ts, N_HEADS, HEAD_DIM))
        hs_l.append(z[:, ts:, :])

    def untranspose(stacked):
        return jnp.transpose(stacked.reshape(depth, bp, N_HEADS, HEAD_DIM, sp), (0, 1, 4, 2, 3))

    return (xp.reshape(bp, sp, D_MODEL), xs.reshape(bs, ts, D_MODEL),
            untranspose(k_all), untranspose(v_all), jnp.stack(hp_l),
            jnp.stack(ks_l), jnp.stack(vs_l), jnp.stack(hs_l))
```

```python
import functools

import jax
import jax.numpy as jnp
from jax import lax
from jax.experimental import pallas as pl
from jax.experimental.pallas import tpu as pltpu

F32 = jnp.float32
BF16 = jnp.bfloat16

D_MODEL = 1024
N_HEADS = 8
HEAD_DIM = 64
ATTN_WIDTH = N_HEADS * HEAD_DIM
MOBA_BLOCK = 256
MOBA_TOPK = 3
PAGE_SIZE = 128
PAGES_PER_BLOCK = MOBA_BLOCK // PAGE_SIZE
POOL_WINDOWS = (2, 4, 8, 16)
POOL_WIDTH = 512
POOL_GROUP_WIDTH = 128
POOL_HIST = 15
HIST_PAD = 16
D_FF = 4 * D_MODEL
ADA_CHUNKS = 6
EPS = 1e-6
NEG_INF = -1e30
SM_SCALE = HEAD_DIM ** -0.5
LOG2_E = 1.4426950408889634
V_ROWS = HEAD_DIM + 16

V7X_VMEM_LIMIT = 56 * 1024 * 1024

TOKEN_TILE = 512
STREAM_TILE = 256
SUB_TILE = 256
SCORE_UNITS_IN_FLIGHT = 4
STREAM_CHUNKS = 4


def _cparams(*sem):
    return pltpu.CompilerParams(dimension_semantics=sem, vmem_limit_bytes=V7X_VMEM_LIMIT)


def _rms(x, g):
    return x * lax.rsqrt(jnp.mean(x * x, axis=-1, keepdims=True) + EPS) * g


def _sigmoid(x):
    return 1.0 / (1.0 + jnp.exp(-x))


def _bdot(a, b):
    return jnp.dot(a, b, preferred_element_type=F32)


def _ada_kernel(c_ref, w_ref, b_ref, o_ref):
    o_ref[0] = _bdot(c_ref[...].astype(BF16), w_ref[0].astype(BF16)) + b_ref[0]


def _ada(c_all, w_ada, b_ada):
    depth, _, width = w_ada.shape
    n = c_all.shape[0]
    tn = 1536
    return pl.pallas_call(
        _ada_kernel,
        grid=(depth, width // tn),
        in_specs=[pl.BlockSpec((n, D_MODEL), lambda l, j: (0, 0)),
                  pl.BlockSpec((1, D_MODEL, tn), lambda l, j: (l, 0, j)),
                  pl.BlockSpec((1, 1, tn), lambda l, j: (l, 0, j))],
        out_specs=pl.BlockSpec((1, n, tn), lambda l, j: (l, 0, j)),
        out_shape=jax.ShapeDtypeStruct((depth, n, width), F32),
        compiler_params=_cparams("arbitrary", "arbitrary"),
        name="ada",
    )(c_all, w_ada, b_ada.reshape(depth, 1, width))


def _mod_spec(mod, chunk, tiles_per_group):
    rows = mod.shape[1]
    return pl.BlockSpec((1, rows, D_MODEL), lambda i: (i // tiles_per_group, 0, chunk))


def _modulated(x_ref, g_ref, sc_ref, sh_ref):
    return (_rms(x_ref[...], g_ref[...]) * (1.0 + sc_ref[0]) + sh_ref[0]).astype(BF16)


def _inproj_kernel(x_ref, sc_ref, sh_ref, g_ref, wu_ref, wqkv_ref, u_ref, qT_ref, kT_ref, vT_ref):
    h = _modulated(x_ref, g_ref, sc_ref, sh_ref)
    u_ref[...] = _bdot(h, wu_ref[...])
    qkvT = lax.dot_general(wqkv_ref[...], h, (((1,), (1,)), ((), ())), preferred_element_type=F32)
    qT_ref[0] = qkvT[0:ATTN_WIDTH]
    kT_ref[0] = qkvT[ATTN_WIDTH:2 * ATTN_WIDTH]
    vT_ref[0] = qkvT[2 * ATTN_WIDTH:3 * ATTN_WIDTH]


def _inproj(x, mod, g, wu, wqkvT, *, tm, seq):
    n_tok = x.shape[0]
    tpg = seq // tm
    n_seq = n_tok // seq
    const = lambda i: (0, 0)
    tok = lambda w: pl.BlockSpec((tm, w), lambda i: (i, 0))
    tspec = pl.BlockSpec((1, ATTN_WIDTH, tm), lambda i: (i // tpg, 0, i % tpg))
    tshape = jax.ShapeDtypeStruct((n_seq, ATTN_WIDTH, seq), F32)
    return pl.pallas_call(
        _inproj_kernel,
        grid=(n_tok // tm,),
        in_specs=[tok(D_MODEL), _mod_spec(mod, 1, tpg), _mod_spec(mod, 0, tpg),
                  pl.BlockSpec((1, D_MODEL), const),
                  pl.BlockSpec(wu.shape, const), pl.BlockSpec(wqkvT.shape, const)],
        out_specs=[tok(POOL_WIDTH), tspec, tspec, tspec],
        out_shape=[jax.ShapeDtypeStruct((n_tok, POOL_WIDTH), F32), tshape, tshape, tshape],
        compiler_params=_cparams("arbitrary"),
        name="inproj",
    )(x, mod, mod, g, wu, wqkvT)


def _pool_stage(u_ref, z_ref, tile_in_seq):
    tm = u_ref.shape[0]

    @pl.when(tile_in_seq == 0)
    def _():
        z_ref[0:HIST_PAD, :] = jnp.zeros((HIST_PAD, POOL_WIDTH), F32)

    @pl.when(tile_in_seq > 0)
    def _():
        z_ref[0:HIST_PAD, :] = z_ref[tm:tm + HIST_PAD, :]

    z_ref[HIST_PAD:HIST_PAD + tm, :] = u_ref[...]


def _pool_rows(z_ref, row0, n_rows, pos0):
    pos = pos0 + lax.broadcasted_iota(jnp.int32, (n_rows, POOL_GROUP_WIDTH), 0)
    parts = []
    for g, w in enumerate(POOL_WINDOWS):
        cols = slice(g * POOL_GROUP_WIDTH, (g + 1) * POOL_GROUP_WIDTH)
        base = HIST_PAD + row0
        cur = z_ref[base:base + n_rows, cols]
        acc = cur
        for r in range(1, w):
            acc = acc + z_ref[base - r:base - r + n_rows, cols]
        cnt = jnp.minimum(pos + 1, w).astype(F32)
        parts.append((acc / cnt - cur).astype(BF16))
    return parts


def _pool_sample_kernel(z_ref, d_ref):
    n_new = d_ref.shape[0]
    for t in range(n_new):
        for g, w in enumerate(POOL_WINDOWS):
            cols = slice(g * POOL_GROUP_WIDTH, (g + 1) * POOL_GROUP_WIDTH)
            cur = z_ref[POOL_HIST + t, :, cols]
            acc = cur
            for r in range(1, w):
                acc = acc + z_ref[POOL_HIST + t - r, :, cols]
            d_ref[t, :, cols] = (acc / float(w) - cur).astype(BF16)


def _pool_sample(z_tm, n_new):
    _, bd, _ = z_tm.shape
    return pl.pallas_call(
        _pool_sample_kernel,
        out_shape=jax.ShapeDtypeStruct((n_new, bd, POOL_WIDTH), BF16),
        name="pool_sample",
    )(z_tm)


def _moba_kernel(qT_ref, kT_ref, vT_ref, o_ref, kb_ref, vb_ref, qm_ref, sel_ref, st_ref, p_ref):
    hp = 2 * HEAD_DIM
    s_len = qT_ref.shape[2]
    nb = s_len // MOBA_BLOCK
    qT = qT_ref[0]
    k = kT_ref[0].T
    kmean = jnp.sum(k.reshape(nb, MOBA_BLOCK, hp), axis=1) * (1.0 / MOBA_BLOCK)
    kb_ref[...] = k.astype(BF16)
    ones = jnp.ones((V_ROWS - HEAD_DIM, s_len), BF16)
    for hh in range(2):
        vb_ref[hh, 0:HEAD_DIM, :] = vT_ref[0, hh * HEAD_DIM:(hh + 1) * HEAD_DIM, :].astype(BF16)
        vb_ref[hh, HEAD_DIM:V_ROWS, :] = ones

    row_head = lax.broadcasted_iota(jnp.int32, (hp, s_len), 0) // HEAD_DIM
    lane_head = lax.broadcasted_iota(jnp.int32, (nb, hp), 1) // HEAD_DIM
    blk = lax.broadcasted_iota(jnp.int32, (nb, s_len), 0)
    n_past = lax.broadcasted_iota(jnp.int32, (nb, s_len), 1) // MOBA_BLOCK
    valid = blk < n_past
    for hh in range(2):
        qm = jnp.where(row_head == hh, qT, 0.0)
        for i in range(nb):
            qm_ref[hh, i] = (qm[:, i * MOBA_BLOCK:(i + 1) * MOBA_BLOCK] * (SM_SCALE * LOG2_E)).astype(BF16)
        sc = jnp.dot(jnp.where(lane_head == hh, kmean, 0.0), qT, preferred_element_type=F32,
                     precision=lax.Precision.HIGHEST)
        sc = jnp.where(valid, sc, NEG_INF)
        rank = jnp.zeros((nb, s_len), F32)
        for m in range(nb):
            row = sc[m:m + 1, :]
            beats = (row > sc) | ((row == sc) & (blk > m))
            rank = rank + jnp.where(beats, 1.0, 0.0)
        sel_ref[hh] = jnp.where(valid & (rank < float(MOBA_TOPK)), 1.0, 0.0)

    key_pos = lax.broadcasted_iota(jnp.int32, (MOBA_BLOCK, MOBA_BLOCK), 0)
    q_pos = lax.broadcasted_iota(jnp.int32, (MOBA_BLOCK, MOBA_BLOCK), 1)
    causal = key_pos <= q_pos
    units = [(i, hh) for i in range(nb) for hh in range(2)]
    state = {}

    def blocks(j):
        return slice(j * MOBA_BLOCK, (j + 1) * MOBA_BLOCK)

    def scores(n):
        i, hh = units[n]
        picked = [sel_ref[hh, j:j + 1, blocks(i)] > 0.0 for j in range(i)]
        st_ref[n % SCORE_UNITS_IN_FLIGHT, 0:(i + 1) * MOBA_BLOCK, :] = _bdot(kb_ref[0:(i + 1) * MOBA_BLOCK, :], qm_ref[hh, i])
        m_run = jnp.max(jnp.where(causal, st_ref[n % SCORE_UNITS_IN_FLIGHT, blocks(i), :], NEG_INF), axis=0, keepdims=True)
        for j in range(i):
            cm = jnp.max(st_ref[n % SCORE_UNITS_IN_FLIGHT, blocks(j), :], axis=0, keepdims=True)
            m_run = jnp.maximum(m_run, jnp.where(picked[j], cm, NEG_INF))
        state[n] = (m_run, picked)

    def values(n):
        i, hh = units[n]
        m_run, picked = state.pop(n)
        for j in range(i + 1):
            st = st_ref[n % SCORE_UNITS_IN_FLIGHT, blocks(j), :]
            if j == i:
                st = jnp.where(causal, st, NEG_INF)
                shift = m_run
            else:
                shift = jnp.where(picked[j], m_run, -NEG_INF)
            p_ref[n % 2, blocks(j), :] = jnp.exp2(st - shift).astype(BF16)
        acc = _bdot(vb_ref[hh, :, 0:(i + 1) * MOBA_BLOCK], p_ref[n % 2, 0:(i + 1) * MOBA_BLOCK, :])
        state[("out", i, hh)] = acc[0:HEAD_DIM] / acc[HEAD_DIM:HEAD_DIM + 1]
        if hh == 1:
            both = jnp.concatenate([state.pop(("out", i, 0)), state.pop(("out", i, 1))], axis=0)
            o_ref[0, blocks(i), :] = both.T.astype(BF16)

    ahead = SCORE_UNITS_IN_FLIGHT - 1
    for n in range(ahead):
        scores(n)
    for n in range(len(units)):
        if n + ahead < len(units):
            scores(n + ahead)
        values(n)


def _moba_prompt(qT, kT, vT):
    b, _, s = qT.shape
    hp = 2 * HEAD_DIM
    nb = s // MOBA_BLOCK
    spec = pl.BlockSpec((1, hp, s), lambda i, p: (i, p, 0))
    return pl.pallas_call(
        _moba_kernel,
        grid=(b, ATTN_WIDTH // hp),
        in_specs=[spec, spec, spec],
        out_specs=pl.BlockSpec((1, s, hp), lambda i, p: (i, 0, p)),
        out_shape=jax.ShapeDtypeStruct((b, s, ATTN_WIDTH), BF16),
        scratch_shapes=[pltpu.VMEM((s, hp), BF16),
                        pltpu.VMEM((2, V_ROWS, s), BF16),
                        pltpu.VMEM((2, nb, hp, MOBA_BLOCK), BF16),
                        pltpu.VMEM((2, nb, s), F32),
                        pltpu.VMEM((SCORE_UNITS_IN_FLIGHT, s, MOBA_BLOCK), F32),
                        pltpu.VMEM((2, s, MOBA_BLOCK), BF16)],
        compiler_params=_cparams("arbitrary", "arbitrary"),
        name="moba_prompt",
    )(qT, kT, vT)


def _cache_stream(pt_ref, ck_ref, buf_ref, sem_ref, *, n_seq, n_pages, pages_per_step):
    cp = pages_per_step // STREAM_CHUNKS

    def page_copy(layer, phys, k, p):
        return pltpu.make_async_copy(ck_ref.at[layer, phys], buf_ref.at[k, p], sem_ref.at[k])

    def issue(st, k):
        first = st * pages_per_step + k * cp
        layer = first // (n_seq * n_pages)
        seq = (first // n_pages) % n_seq
        page0 = first % n_pages
        for p in range(cp):
            page_copy(layer, pt_ref[seq, page0 + p], k, p).start()

    def point(step, k, acc):
        for p in range(cp):
            page_copy(0, 0, k, p).wait()
        if k == 0:
            issue(step, STREAM_CHUNKS - 1)
        else:
            @pl.when(step + 1 < pl.num_programs(0))
            def _():
                issue(step + 1, k - 1)
        lane = lax.broadcasted_iota(jnp.int32, (ATTN_WIDTH, PAGE_SIZE), 1)
        block0 = ((step * pages_per_step + k * cp) % n_pages) // PAGES_PER_BLOCK
        for b2 in range(cp // PAGES_PER_BLOCK):
            t = buf_ref[k, PAGES_PER_BLOCK * b2]
            for e in range(1, PAGES_PER_BLOCK):
                t = t + buf_ref[k, PAGES_PER_BLOCK * b2 + e]
            acc = jnp.where(lane == block0 + b2, jnp.sum(t, axis=1, keepdims=True), acc)
        return acc

    return issue, point


def _tail_kernel(*refs, pool_in_kernel, tiles_per_seq, stream, stack, n_steps):
    (x_ref, p_ref, at_ref, sc1_ref, sh1_ref, gt1_ref, sc2_ref, sh2_ref, gt2_ref,
     gpre_ref, gpost_ref, gfpre_ref, gfpost_ref, wg_ref, lin_ref, ps_ref, wbp_ref, wba_ref, wout_ref,
     w1_ref, w2_ref) = refs[:21]
    rest = list(refs[21:])
    if stream is not None:
        pt_ref, ck_ref = rest[:2]
        rest = rest[2:]
    n_groups, per_group = stack
    n_src = n_groups * per_group
    stack_src, rest = rest[:n_src], rest[n_src:]
    o_ref = rest.pop(0)
    if stream is not None:
        km_ref = rest.pop(0)
    stack_dst, rest = rest[:n_groups], rest[n_groups:]
    z_ref = rest.pop(0) if pool_in_kernel else None
    stack_sem = rest.pop() if n_src else None
    stage_ref = rest.pop() if n_src else None
    step = pl.program_id(0)

    def stack_copies(direction):
        copies = []
        for n, s in enumerate(stack_src):
            g, l = divmod(n, per_group)
            rows_per_step = s.shape[0] // n_steps
            rows = pl.ds(pl.multiple_of(step * rows_per_step, 8), rows_per_step)
            if direction == 0:
                copies.append(pltpu.make_async_copy(s.at[rows, :], stage_ref.at[n], stack_sem.at[0, n]))
            else:
                copies.append(pltpu.make_async_copy(stage_ref.at[n], stack_dst[g].at[l, rows, :], stack_sem.at[1, n]))
        return copies

    for cp in stack_copies(0):
        cp.start()

    def stream_point(k):
        pass

    if stream is not None:
        buf_ref, sem_ref, kacc_ref = rest
        issue, point = _cache_stream(pt_ref, ck_ref, buf_ref, sem_ref, **stream)

        @pl.when(step == 0)
        def _():
            for k in range(STREAM_CHUNKS - 1):
                issue(step, k)

        @pl.when((step * stream["pages_per_step"]) % stream["n_pages"] == 0)
        def _():
            kacc_ref[...] = jnp.zeros(kacc_ref.shape, F32)

        def stream_point(k):
            acc = point(step, k, kacc_ref[...])
            kacc_ref[...] = acc
            if k == STREAM_CHUNKS - 1:
                km_ref[0, 0] = acc * (1.0 / MOBA_BLOCK)

    tm = x_ref.shape[0]
    sub = min(tm, SUB_TILE)
    tile_in_seq = step % tiles_per_seq
    if pool_in_kernel:
        _pool_stage(p_ref, z_ref, tile_in_seq)

    def mod_rows(ref, rows):
        return ref[0] if ref.shape[1] == 1 else ref[0, rows, :]

    for s in range(tm // sub):
        rows = slice(s * sub, (s + 1) * sub)
        here = stream_point if s == 0 else (lambda k: None)
        here(0)
        x = x_ref[rows, :]
        h = (_rms(x, gpre_ref[...]) * (1.0 + mod_rows(sc1_ref, rows)) + mod_rows(sh1_ref, rows)).astype(BF16)
        gate_pool = _sigmoid(_bdot(h, wg_ref[:, 0:D_MODEL]))
        gate_attn = _sigmoid(_bdot(h, wg_ref[:, D_MODEL:2 * D_MODEL]))
        if pool_in_kernel:
            d_parts = _pool_rows(z_ref, s * sub, sub, tile_in_seq * tm + s * sub)
        else:
            d_parts = [p_ref[rows, g * POOL_GROUP_WIDTH:(g + 1) * POOL_GROUP_WIDTH] for g in range(len(POOL_WINDOWS))]
        mixed = jnp.concatenate([_bdot(d_parts[g], lin_ref[g]) for g in range(len(POOL_WINDOWS))],
                                axis=-1) * ps_ref[...]
        merged = (gate_pool * _bdot(mixed.astype(BF16), wbp_ref[...])
                  + gate_attn * _bdot(at_ref[rows, :], wba_ref[...]))
        here(1)
        x1 = x + mod_rows(gt1_ref, rows) * _rms(_bdot(merged.astype(BF16), wout_ref[...]), gpost_ref[...])
        h2 = (_rms(x1, gfpre_ref[...]) * (1.0 + mod_rows(sc2_ref, rows)) + mod_rows(sh2_ref, rows)).astype(BF16)
        here(2)
        if s == 0:
            for cp in stack_copies(0):
                cp.wait()
            for cp in stack_copies(1):
                cp.start()
        ff = jnp.zeros(x.shape, F32)
        for c in range(D_FF // D_MODEL):
            cols = slice(c * D_MODEL, (c + 1) * D_MODEL)
            a = jnp.maximum(_bdot(h2, w1_ref[:, cols]), 0.0)
            ff = ff + _bdot((a * a).astype(BF16), w2_ref[cols, :])
            if c == 1:
                here(3)
        o_ref[rows, :] = x1 + mod_rows(gt2_ref, rows) * _rms(ff, gfpost_ref[...])

    for cp in stack_copies(1):
        cp.wait()


def _tail(x, pool_in, attn, mod, gains, weights, *, tm, seq, pool_in_kernel, cache=None, stack=()):
    n_tok = x.shape[0]
    n_steps = n_tok // tm
    tpg = seq // tm
    tok = lambda w: pl.BlockSpec((tm, w), lambda i: (i, 0))
    full = lambda a: pl.BlockSpec(a.shape, lambda i: (0,) * a.ndim, pipeline_mode=pl.Buffered(1))
    mods = [_mod_spec(mod, c, tpg) for c in (1, 0, 2, 4, 3, 5)]
    in_specs = ([tok(D_MODEL), tok(POOL_WIDTH), tok(ATTN_WIDTH)] + mods
                + [full(a) for a in gains] + [full(a) for a in weights])
    operands = [x, pool_in, attn, *([mod] * 6), *gains, *weights]
    out_specs = [tok(D_MODEL)]
    out_shape = [jax.ShapeDtypeStruct((n_tok, D_MODEL), F32)]
    scratch = [pltpu.VMEM((HIST_PAD + tm, POOL_WIDTH), F32)] if pool_in_kernel else []
    stream = None
    if cache is not None:
        page_table, cacheT_k = cache
        depth = cacheT_k.shape[0]
        n_seq, n_pages = page_table.shape
        pages_per_step, rem = divmod(depth * n_seq * n_pages, n_steps)
        assert rem == 0 and n_pages % pages_per_step == 0 and pages_per_step % (STREAM_CHUNKS * PAGES_PER_BLOCK) == 0
        assert n_pages // PAGES_PER_BLOCK <= PAGE_SIZE
        stream = dict(n_seq=n_seq, n_pages=n_pages, pages_per_step=pages_per_step)
        in_specs += [pl.BlockSpec(memory_space=pltpu.SMEM), pl.BlockSpec(memory_space=pl.ANY)]
        operands += [page_table, cacheT_k]
        unit = lambda i: (i * pages_per_step) // n_pages
        out_specs.append(pl.BlockSpec((1, 1, ATTN_WIDTH, PAGE_SIZE), lambda i: (unit(i) // n_seq, unit(i) % n_seq, 0, 0)))
        out_shape.append(jax.ShapeDtypeStruct((depth, n_seq, ATTN_WIDTH, PAGE_SIZE), F32))
        scratch += [pltpu.VMEM((STREAM_CHUNKS, pages_per_step // STREAM_CHUNKS, ATTN_WIDTH, PAGE_SIZE), F32),
                    pltpu.SemaphoreType.DMA((STREAM_CHUNKS,)),
                    pltpu.VMEM((ATTN_WIDTH, PAGE_SIZE), F32)]
    per_group = len(stack[0]) if stack else 0
    if stack:
        hbm = pl.BlockSpec(memory_space=pl.ANY)
        flat = [a for group in stack for a in group]
        assert all(len(g) == per_group for g in stack) and all(a.shape[0] % (8 * n_steps) == 0 for a in flat)
        in_specs += [hbm] * len(flat)
        operands += flat
        out_specs += [hbm] * len(stack)
        out_shape += [jax.ShapeDtypeStruct((per_group,) + g[0].shape, g[0].dtype) for g in stack]
        scratch += [pltpu.VMEM((len(flat), flat[0].shape[0] // n_steps, flat[0].shape[1]), flat[0].dtype),
                    pltpu.SemaphoreType.DMA((2, len(flat)))]
    outs = pl.pallas_call(
        functools.partial(_tail_kernel, pool_in_kernel=pool_in_kernel, tiles_per_seq=tpg, stream=stream,
                          stack=(len(stack), per_group), n_steps=n_steps),
        grid=(n_steps,),
        in_specs=in_specs,
        out_specs=out_specs,
        out_shape=out_shape,
        scratch_shapes=scratch,
        compiler_params=_cparams("arbitrary"),
        name="tail",
    )(*operands)
    return outs


def _select_kernel(q_ref, km_ref, o_ref, *, n_blocks):
    sc = jnp.dot(q_ref[0], km_ref[0, 0], preferred_element_type=F32, precision=lax.Precision.HIGHEST)
    lane = lax.broadcasted_iota(jnp.int32, sc.shape, 1)
    cur = jnp.where(lane < n_blocks, sc, -jnp.inf)
    out = jnp.zeros(sc.shape, jnp.int32)
    for r in range(MOBA_TOPK):
        m = jnp.max(cur, axis=1, keepdims=True)
        pick = jnp.min(jnp.where(cur == m, lane, PAGE_SIZE), axis=1, keepdims=True)
        out = jnp.where(lane == r, pick, out)
        cur = jnp.where(lane == pick, -jnp.inf, cur)
    o_ref[0] = out


def _select_sample(q_exp, kmeanT, layer, n_blocks):
    n_seq, rows, _ = q_exp.shape
    return pl.pallas_call(
        functools.partial(_select_kernel, n_blocks=n_blocks),
        grid=(n_seq,),
        in_specs=[pl.BlockSpec((1, rows, ATTN_WIDTH), lambda b: (b, 0, 0)),
                  pl.BlockSpec((1, 1, ATTN_WIDTH, PAGE_SIZE), lambda b: (layer, b, 0, 0))],
        out_specs=pl.BlockSpec((1, rows, PAGE_SIZE), lambda b: (b, 0, 0)),
        out_shape=jax.ShapeDtypeStruct((n_seq, rows, PAGE_SIZE), jnp.int32),
        compiler_params=_cparams("arbitrary"),
        name="select_sample",
    )(q_exp, kmeanT)


def _attend_kernel(idx_ref, pt_ref, qT_ref, kT_ref, vT_ref, ck_ref, cv_ref, o_ref, kbuf_ref, vbuf_ref, sem_ref,
                   *, layer, n_new):
    n_seq, n_pages = pt_ref.shape
    tiles_per_tok = MOBA_TOPK * PAGES_PER_BLOCK
    tiles_per_head = n_new * tiles_per_tok
    b = pl.program_id(0)
    slot = b % 2

    def tile_copies(hh, phys, sl, j):
        rows = pl.ds(pl.multiple_of(hh * HEAD_DIM, HEAD_DIM), HEAD_DIM)
        return (pltpu.make_async_copy(ck_ref.at[layer, phys, rows, :], kbuf_ref.at[sl, j], sem_ref.at[sl, 0]),
                pltpu.make_async_copy(cv_ref.at[layer, phys, rows, :], vbuf_ref.at[sl, j], sem_ref.at[sl, 1]))

    def issue(bb, sl):
        def per_head(hh, carry):
            base = (bb * N_HEADS + hh) * (n_new * MOBA_TOPK)
            for t in range(n_new):
                for r in range(MOBA_TOPK):
                    blk = idx_ref[base + t * MOBA_TOPK + r]
                    for pg in range(PAGES_PER_BLOCK):
                        logical = jnp.minimum(blk * PAGES_PER_BLOCK + pg, n_pages - 1)
                        j = hh * tiles_per_head + (t * MOBA_TOPK + r) * PAGES_PER_BLOCK + pg
                        ck, cv = tile_copies(hh, pt_ref[bb, logical], sl, j)
                        ck.start()
                        cv.start()
            return carry

        lax.fori_loop(0, N_HEADS, per_head, 0)

    @pl.when(b == 0)
    def _():
        issue(b, slot)
        o_ref[...] = jnp.zeros(o_ref.shape, F32)

    @pl.when(b + 1 < pl.num_programs(0))
    def _():
        issue(b + 1, 1 - slot)

    for j in range(N_HEADS * tiles_per_head):
        ck, cv = tile_copies(0, 0, slot, j)
        ck.wait()
        cv.wait()

    lane = lax.broadcasted_iota(jnp.int32, (1, qT_ref.shape[1]), 1)
    for hh in range(N_HEADS):
        rows = slice(hh * HEAD_DIM, (hh + 1) * HEAD_DIM)
        qT = qT_ref[rows, :]
        k_new = kT_ref[rows, :]
        v_new = vT_ref[rows, :]
        out = o_ref[rows, :]
        toks = range(n_new)
        col = [b * n_new + t for t in toks]
        first = [hh * tiles_per_head + t * tiles_per_tok for t in toks]
        qcol = [jnp.sum(jnp.where(lane == col[t], qT, 0.0), axis=1, keepdims=True) * SM_SCALE
                for t in toks]
        scores = [[jnp.sum(kbuf_ref[slot, first[t] + j] * qcol[t], axis=0, keepdims=True)
                   for j in range(tiles_per_tok)] for t in toks]
        for t in toks:
            own_ok = (lane >= b * n_new) & (lane <= col[t])
            scores[t].append(jnp.where(own_ok, jnp.sum(k_new * qcol[t], axis=0, keepdims=True), NEG_INF))
        m = []
        for t in toks:
            mt = scores[t][0]
            for s in scores[t][1:]:
                mt = jnp.maximum(mt, s)
            m.append(jnp.max(mt, axis=1, keepdims=True))
        probs = [[jnp.exp(s - m[t]) for s in scores[t]] for t in toks]
        denom = []
        for t in toks:
            tot = probs[t][0]
            for p in probs[t][1:]:
                tot = tot + p
            denom.append(jnp.sum(tot, axis=1, keepdims=True))
        for t in toks:
            pv = v_new * probs[t][-1]
            for j in range(tiles_per_tok):
                pv = pv + vbuf_ref[slot, first[t] + j] * probs[t][j]
            o_col = jnp.sum(pv, axis=1, keepdims=True) / denom[t]
            out = jnp.where(lane == col[t], o_col, out)
        o_ref[rows, :] = out


def _attend_sample(idx_flat, page_table, qT, kT, vT, cacheT_k, cacheT_v, layer, n_new):
    n_seq = page_table.shape[0]
    n_tok = qT.shape[1]
    n_tiles = N_HEADS * n_new * MOBA_TOPK * PAGES_PER_BLOCK
    whole = pl.BlockSpec((ATTN_WIDTH, n_tok), lambda b, idx, pt: (0, 0))
    hbm = pl.BlockSpec(memory_space=pl.ANY)
    return pl.pallas_call(
        functools.partial(_attend_kernel, layer=layer, n_new=n_new),
        grid_spec=pltpu.PrefetchScalarGridSpec(
            num_scalar_prefetch=2,
            grid=(n_seq,),
            in_specs=[whole, whole, whole, hbm, hbm],
            out_specs=whole,
            scratch_shapes=[pltpu.VMEM((2, n_tiles, HEAD_DIM, PAGE_SIZE), F32),
                            pltpu.VMEM((2, n_tiles, HEAD_DIM, PAGE_SIZE), F32),
                            pltpu.SemaphoreType.DMA((2, 2))]),
        out_shape=jax.ShapeDtypeStruct((ATTN_WIDTH, n_tok), F32),
        compiler_params=_cparams("arbitrary"),
        name="attend_sample",
    )(idx_flat, page_table, qT, kT, vT, cacheT_k, cacheT_v)


def kernel(x_prompt, x_sample, cache_k, cache_v, state_pool, page_table, c_prompt, c_sample, w_ada, b_ada, g_mix_pre, g_mix_post, w_in, pool_lin, pool_scale, w_branch_pool, w_branch_attn, w_out, g_ffn_pre, g_ffn_post, w_ff1, w_ff2):
    depth = w_ada.shape[0]
    bp, sp, _ = x_prompt.shape
    bs, ts, _ = x_sample.shape
    n_pages = page_table.shape[1]
    n_blocks = n_pages // PAGES_PER_BLOCK
    n_phys = cache_k.shape[1]

    cacheT_k = jnp.transpose(cache_k, (0, 1, 3, 4, 2)).reshape(depth, n_phys, ATTN_WIDTH, PAGE_SIZE)
    cacheT_v = jnp.transpose(cache_v, (0, 1, 3, 4, 2)).reshape(depth, n_phys, ATTN_WIDTH, PAGE_SIZE)

    ada = _ada(jnp.concatenate([c_prompt, c_sample], axis=0), w_ada, b_ada)

    head_mask = (jnp.arange(ATTN_WIDTH)[None, :] // HEAD_DIM == jnp.arange(N_HEADS)[:, None]).astype(F32)

    xp = x_prompt.reshape(bp * sp, D_MODEL)
    xs = x_sample.reshape(bs * ts, D_MODEL)
    kp_l, vp_l, hp_l, ks_l, vs_l, hs_l = [], [], [], [], [], []
    for l in range(depth):
        wu = w_in[l, :, :POOL_WIDTH].astype(BF16)
        wqkvT = w_in[l, :, POOL_WIDTH:POOL_WIDTH + 3 * ATTN_WIDTH].T.astype(BF16)
        wg = w_in[l, :, POOL_WIDTH + 3 * ATTN_WIDTH:].astype(BF16)
        lin = pool_lin[l].astype(BF16)
        ps = pool_scale[l].reshape(1, POOL_WIDTH)
        wbp = w_branch_pool[l].astype(BF16)
        wba = w_branch_attn[l].astype(BF16)
        wo = w_out[l].astype(BF16)
        w1 = w_ff1[l].astype(BF16)
        w2 = w_ff2[l].astype(BF16)
        gpre = g_mix_pre[l].reshape(1, D_MODEL)
        gpost = g_mix_post[l].reshape(1, D_MODEL)
        gfpre = g_ffn_pre[l].reshape(1, D_MODEL)
        gfpost = g_ffn_post[l].reshape(1, D_MODEL)
        gains = (gpre, gpost, gfpre, gfpost)
        weights = (wg, lin, ps, wbp, wba, wo, w1, w2)
        mod_p = ada[l, :bp].reshape(bp, 1, ADA_CHUNKS * D_MODEL)
        mod_s = jnp.repeat(ada[l, bp:], ts, axis=0).reshape(1, bs * ts, ADA_CHUNKS * D_MODEL)

        u, qT, kT, vT = _inproj(xp, mod_p, gpre, wu, wqkvT, tm=TOKEN_TILE, seq=sp)
        attn = _moba_prompt(qT, kT, vT).reshape(bp * sp, ATTN_WIDTH)
        kp_l.append(kT.reshape(bp * ATTN_WIDTH, sp))
        vp_l.append(vT.reshape(bp * ATTN_WIDTH, sp))
        if l == 0:
            xp, kmeanT = _tail(xp, u, attn, mod_p, gains, weights, tm=STREAM_TILE, seq=sp, pool_in_kernel=True,
                               cache=(page_table, cacheT_k))
        elif l == depth - 1:
            xp, k_all, v_all = _tail(xp, u, attn, mod_p, gains, weights, tm=TOKEN_TILE, seq=sp, pool_in_kernel=True,
                                     stack=(kp_l, vp_l))
        else:
            xp, = _tail(xp, u, attn, mod_p, gains, weights, tm=TOKEN_TILE, seq=sp, pool_in_kernel=True)
        hp_l.append(u.reshape(bp, sp, POOL_WIDTH)[:, sp - POOL_HIST:, :])

        n_s = bs * ts
        u_s, qT_s, kT_s, vT_s = _inproj(xs, mod_s, gpre, wu, wqkvT, tm=n_s, seq=n_s)
        qT_s, kT_s, vT_s = qT_s[0], kT_s[0], vT_s[0]
        z = jnp.concatenate([state_pool[l], u_s.reshape(bs, ts, POOL_WIDTH)], axis=1)
        d_s = _pool_sample(jnp.transpose(z, (1, 0, 2)), ts)
        d_s = jnp.transpose(d_s, (1, 0, 2)).reshape(n_s, POOL_WIDTH)
        q_s = qT_s.T.reshape(bs, 1, ts, ATTN_WIDTH)
        q_exp = (q_s * head_mask[None, :, None, :]).reshape(bs, N_HEADS * ts, ATTN_WIDTH)
        idx = _select_sample(q_exp, kmeanT, l, n_blocks)[:, :, :MOBA_TOPK]
        attnT_s = _attend_sample(idx.reshape(-1), page_table, qT_s, kT_s, vT_s, cacheT_k, cacheT_v, l, ts)
        attn_s = attnT_s.T.astype(BF16)
        xs, = _tail(xs, d_s, attn_s, mod_s, gains, weights, tm=n_s, seq=n_s, pool_in_kernel=False)
        ks_l.append(kT_s.T.reshape(bs, ts, N_HEADS, HEAD_DIM))
        vs_l.append(vT_s.T.reshape(bs, ts, N_HEADS, HEAD_DIM))
        hs_l.append(z[:, ts:, :])

    def untranspose(stacked):
        return jnp.transpose(stacked.reshape(depth, bp, N_HEADS, HEAD_DIM, sp), (0, 1, 4, 2, 3))

    return (xp.reshape(bp, sp, D_MODEL), xs.reshape(bs, ts, D_MODEL),
            untranspose(k_all), untranspose(v_all), jnp.stack(hp_l),
            jnp.stack(ks_l), jnp.stack(vs_l), jnp.stack(hs_l))
```

```python
import functools

import jax
import jax.numpy as jnp
from jax import lax
from jax.experimental import pallas as pl
from jax.experimental.pallas import tpu as pltpu

F32 = jnp.float32
BF16 = jnp.bfloat16

D_MODEL = 1024
N_HEADS = 8
HEAD_DIM = 64
ATTN_WIDTH = N_HEADS * HEAD_DIM
MOBA_BLOCK = 256
MOBA_TOPK = 3
PAGE_SIZE = 128
PAGES_PER_BLOCK = MOBA_BLOCK // PAGE_SIZE
POOL_WINDOWS = (2, 4, 8, 16)
POOL_WIDTH = 512
POOL_GROUP_WIDTH = 128
POOL_HIST = 15
HIST_PAD = 16
D_FF = 4 * D_MODEL
ADA_CHUNKS = 6
EPS = 1e-6
NEG_INF = -1e30
SM_SCALE = HEAD_DIM ** -0.5
LOG2_E = 1.4426950408889634
V_ROWS = HEAD_DIM + 16

V7X_VMEM_LIMIT = 56 * 1024 * 1024

TOKEN_TILE = 512
STREAM_TILE = 256
SUB_TILE = 256
SCORE_UNITS_IN_FLIGHT = 4
ATTEND_HEAD_GROUP = 2
STREAM_CHUNKS = 4


def _cparams(*sem):
    return pltpu.CompilerParams(dimension_semantics=sem, vmem_limit_bytes=V7X_VMEM_LIMIT)


def _rms(x, g):
    return x * lax.rsqrt(jnp.mean(x * x, axis=-1, keepdims=True) + EPS) * g


def _sigmoid(x):
    return 1.0 / (1.0 + jnp.exp(-x))


def _bdot(a, b):
    return jnp.dot(a, b, preferred_element_type=F32)


def _ada_kernel(c_ref, w_ref, b_ref, o_ref):
    o_ref[0] = _bdot(c_ref[...].astype(BF16), w_ref[0].astype(BF16)) + b_ref[0]


def _ada(c_all, w_ada, b_ada):
    depth, _, width = w_ada.shape
    n = c_all.shape[0]
    tn = 1536
    return pl.pallas_call(
        _ada_kernel,
        grid=(depth, width // tn),
        in_specs=[pl.BlockSpec((n, D_MODEL), lambda l, j: (0, 0)),
                  pl.BlockSpec((1, D_MODEL, tn), lambda l, j: (l, 0, j)),
                  pl.BlockSpec((1, 1, tn), lambda l, j: (l, 0, j))],
        out_specs=pl.BlockSpec((1, n, tn), lambda l, j: (l, 0, j)),
        out_shape=jax.ShapeDtypeStruct((depth, n, width), F32),
        compiler_params=_cparams("arbitrary", "arbitrary"),
        name="ada",
    )(c_all, w_ada, b_ada.reshape(depth, 1, width))


def _mod_spec(mod, chunk, tiles_per_group):
    rows = mod.shape[1]
    return pl.BlockSpec((1, rows, D_MODEL), lambda i: (i // tiles_per_group, 0, chunk))


def _modulated(x_ref, g_ref, sc_ref, sh_ref):
    return (_rms(x_ref[...], g_ref[...]) * (1.0 + sc_ref[0]) + sh_ref[0]).astype(BF16)


def _inproj_kernel(x_ref, sc_ref, sh_ref, g_ref, wu_ref, wqkv_ref, u_ref, qT_ref, kT_ref, vT_ref):
    h = _modulated(x_ref, g_ref, sc_ref, sh_ref)
    u_ref[...] = _bdot(h, wu_ref[...])
    qkvT = lax.dot_general(wqkv_ref[...], h, (((1,), (1,)), ((), ())), preferred_element_type=F32)
    qT_ref[0] = qkvT[0:ATTN_WIDTH]
    kT_ref[0] = qkvT[ATTN_WIDTH:2 * ATTN_WIDTH]
    vT_ref[0] = qkvT[2 * ATTN_WIDTH:3 * ATTN_WIDTH]


def _inproj(x, mod, g, wu, wqkvT, *, tm, seq):
    n_tok = x.shape[0]
    tpg = seq // tm
    n_seq = n_tok // seq
    const = lambda i: (0, 0)
    tok = lambda w: pl.BlockSpec((tm, w), lambda i: (i, 0))
    tspec = pl.BlockSpec((1, ATTN_WIDTH, tm), lambda i: (i // tpg, 0, i % tpg))
    tshape = jax.ShapeDtypeStruct((n_seq, ATTN_WIDTH, seq), F32)
    return pl.pallas_call(
        _inproj_kernel,
        grid=(n_tok // tm,),
        in_specs=[tok(D_MODEL), _mod_spec(mod, 1, tpg), _mod_spec(mod, 0, tpg),
                  pl.BlockSpec((1, D_MODEL), const),
                  pl.BlockSpec(wu.shape, const), pl.BlockSpec(wqkvT.shape, const)],
        out_specs=[tok(POOL_WIDTH), tspec, tspec, tspec],
        out_shape=[jax.ShapeDtypeStruct((n_tok, POOL_WIDTH), F32), tshape, tshape, tshape],
        compiler_params=_cparams("arbitrary"),
        name="inproj",
    )(x, mod, mod, g, wu, wqkvT)


def _pool_stage(u_ref, z_ref, tile_in_seq):
    tm = u_ref.shape[0]

    @pl.when(tile_in_seq == 0)
    def _():
        z_ref[0:HIST_PAD, :] = jnp.zeros((HIST_PAD, POOL_WIDTH), F32)

    @pl.when(tile_in_seq > 0)
    def _():
        z_ref[0:HIST_PAD, :] = z_ref[tm:tm + HIST_PAD, :]

    z_ref[HIST_PAD:HIST_PAD + tm, :] = u_ref[...]


def _pool_rows(z_ref, row0, n_rows, pos0):
    pos = pos0 + lax.broadcasted_iota(jnp.int32, (n_rows, POOL_GROUP_WIDTH), 0)
    parts = []
    for g, w in enumerate(POOL_WINDOWS):
        cols = slice(g * POOL_GROUP_WIDTH, (g + 1) * POOL_GROUP_WIDTH)
        base = HIST_PAD + row0
        cur = z_ref[base:base + n_rows, cols]
        acc = cur
        for r in range(1, w):
            acc = acc + z_ref[base - r:base - r + n_rows, cols]
        cnt = jnp.minimum(pos + 1, w).astype(F32)
        parts.append((acc / cnt - cur).astype(BF16))
    return parts


def _pool_sample_kernel(z_ref, d_ref):
    n_new = d_ref.shape[0]
    for t in range(n_new):
        for g, w in enumerate(POOL_WINDOWS):
            cols = slice(g * POOL_GROUP_WIDTH, (g + 1) * POOL_GROUP_WIDTH)
            cur = z_ref[POOL_HIST + t, :, cols]
            acc = cur
            for r in range(1, w):
                acc = acc + z_ref[POOL_HIST + t - r, :, cols]
            d_ref[t, :, cols] = (acc / float(w) - cur).astype(BF16)


def _pool_sample(z_tm, n_new):
    _, bd, _ = z_tm.shape
    return pl.pallas_call(
        _pool_sample_kernel,
        out_shape=jax.ShapeDtypeStruct((n_new, bd, POOL_WIDTH), BF16),
        name="pool_sample",
    )(z_tm)


def _moba_kernel(qT_ref, kT_ref, vT_ref, o_ref, kb_ref, vb_ref, qm_ref, sel_ref, st_ref, p_ref):
    hp = 2 * HEAD_DIM
    s_len = qT_ref.shape[2]
    nb = s_len // MOBA_BLOCK
    qT = qT_ref[0]
    k = kT_ref[0].T
    kmean = jnp.sum(k.reshape(nb, MOBA_BLOCK, hp), axis=1) * (1.0 / MOBA_BLOCK)
    kb_ref[...] = k.astype(BF16)
    ones = jnp.ones((V_ROWS - HEAD_DIM, s_len), BF16)
    for hh in range(2):
        vb_ref[hh, 0:HEAD_DIM, :] = vT_ref[0, hh * HEAD_DIM:(hh + 1) * HEAD_DIM, :].astype(BF16)
        vb_ref[hh, HEAD_DIM:V_ROWS, :] = ones

    row_head = lax.broadcasted_iota(jnp.int32, (hp, s_len), 0) // HEAD_DIM
    lane_head = lax.broadcasted_iota(jnp.int32, (nb, hp), 1) // HEAD_DIM
    blk = lax.broadcasted_iota(jnp.int32, (nb, s_len), 0)
    n_past = lax.broadcasted_iota(jnp.int32, (nb, s_len), 1) // MOBA_BLOCK
    valid = blk < n_past
    for hh in range(2):
        qm = jnp.where(row_head == hh, qT, 0.0)
        for i in range(nb):
            qm_ref[hh, i] = (qm[:, i * MOBA_BLOCK:(i + 1) * MOBA_BLOCK] * (SM_SCALE * LOG2_E)).astype(BF16)
        sc = jnp.dot(jnp.where(lane_head == hh, kmean, 0.0), qT, preferred_element_type=F32,
                     precision=lax.Precision.HIGHEST)
        sc = jnp.where(valid, sc, NEG_INF)
        rank = jnp.zeros((nb, s_len), F32)
        for m in range(nb):
            row = sc[m:m + 1, :]
            beats = (row > sc) | ((row == sc) & (blk > m))
            rank = rank + jnp.where(beats, 1.0, 0.0)
        sel_ref[hh] = jnp.where(valid & (rank < float(MOBA_TOPK)), 1.0, 0.0)

    key_pos = lax.broadcasted_iota(jnp.int32, (MOBA_BLOCK, MOBA_BLOCK), 0)
    q_pos = lax.broadcasted_iota(jnp.int32, (MOBA_BLOCK, MOBA_BLOCK), 1)
    causal = key_pos <= q_pos
    units = [(i, hh) for i in range(nb) for hh in range(2)]
    state = {}

    def blocks(j):
        return slice(j * MOBA_BLOCK, (j + 1) * MOBA_BLOCK)

    def scores(n):
        i, hh = units[n]
        picked = [sel_ref[hh, j:j + 1, blocks(i)] > 0.0 for j in range(i)]
        st_ref[n % SCORE_UNITS_IN_FLIGHT, 0:(i + 1) * MOBA_BLOCK, :] = _bdot(kb_ref[0:(i + 1) * MOBA_BLOCK, :], qm_ref[hh, i])
        m_run = jnp.max(jnp.where(causal, st_ref[n % SCORE_UNITS_IN_FLIGHT, blocks(i), :], NEG_INF), axis=0, keepdims=True)
        for j in range(i):
            cm = jnp.max(st_ref[n % SCORE_UNITS_IN_FLIGHT, blocks(j), :], axis=0, keepdims=True)
            m_run = jnp.maximum(m_run, jnp.where(picked[j], cm, NEG_INF))
        state[n] = (m_run, picked)

    def values(n):
        i, hh = units[n]
        m_run, picked = state.pop(n)
        for j in range(i + 1):
            st = st_ref[n % SCORE_UNITS_IN_FLIGHT, blocks(j), :]
            if j == i:
                st = jnp.where(causal, st, NEG_INF)
                shift = m_run
            else:
                shift = jnp.where(picked[j], m_run, -NEG_INF)
            p_ref[n % 2, blocks(j), :] = jnp.exp2(st - shift).astype(BF16)
        acc = _bdot(vb_ref[hh, :, 0:(i + 1) * MOBA_BLOCK], p_ref[n % 2, 0:(i + 1) * MOBA_BLOCK, :])
        state[("out", i, hh)] = acc[0:HEAD_DIM] / acc[HEAD_DIM:HEAD_DIM + 1]
        if hh == 1:
            both = jnp.concatenate([state.pop(("out", i, 0)), state.pop(("out", i, 1))], axis=0)
            o_ref[0, blocks(i), :] = both.T.astype(BF16)

    ahead = SCORE_UNITS_IN_FLIGHT - 1
    for n in range(ahead):
        scores(n)
    for n in range(len(units)):
        if n + ahead < len(units):
            scores(n + ahead)
        values(n)


def _moba_prompt(qT, kT, vT):
    b, _, s = qT.shape
    hp = 2 * HEAD_DIM
    nb = s // MOBA_BLOCK
    spec = pl.BlockSpec((1, hp, s), lambda i, p: (i, p, 0))
    return pl.pallas_call(
        _moba_kernel,
        grid=(b, ATTN_WIDTH // hp),
        in_specs=[spec, spec, spec],
        out_specs=pl.BlockSpec((1, s, hp), lambda i, p: (i, 0, p)),
        out_shape=jax.ShapeDtypeStruct((b, s, ATTN_WIDTH), BF16),
        scratch_shapes=[pltpu.VMEM((s, hp), BF16),
                        pltpu.VMEM((2, V_ROWS, s), BF16),
                        pltpu.VMEM((2, nb, hp, MOBA_BLOCK), BF16),
                        pltpu.VMEM((2, nb, s), F32),
                        pltpu.VMEM((SCORE_UNITS_IN_FLIGHT, s, MOBA_BLOCK), F32),
                        pltpu.VMEM((2, s, MOBA_BLOCK), BF16)],
        compiler_params=_cparams("arbitrary", "arbitrary"),
        name="moba_prompt",
    )(qT, kT, vT)


def _cache_stream(pt_ref, ck_ref, buf_ref, sem_ref, *, n_seq, n_pages, pages_per_step):
    cp = pages_per_step // STREAM_CHUNKS

    def page_copy(layer, phys, k, p):
        return pltpu.make_async_copy(ck_ref.at[layer, phys], buf_ref.at[k, p], sem_ref.at[k])

    def issue(st, k):
        first = st * pages_per_step + k * cp
        layer = first // (n_seq * n_pages)
        seq = (first // n_pages) % n_seq
        page0 = first % n_pages
        for p in range(cp):
            page_copy(layer, pt_ref[seq, page0 + p], k, p).start()

    def point(step, k, acc):
        for p in range(cp):
            page_copy(0, 0, k, p).wait()
        if k == 0:
            issue(step, STREAM_CHUNKS - 1)
        else:
            @pl.when(step + 1 < pl.num_programs(0))
            def _():
                issue(step + 1, k - 1)
        lane = lax.broadcasted_iota(jnp.int32, (ATTN_WIDTH, PAGE_SIZE), 1)
        block0 = ((step * pages_per_step + k * cp) % n_pages) // PAGES_PER_BLOCK
        for b2 in range(cp // PAGES_PER_BLOCK):
            t = buf_ref[k, PAGES_PER_BLOCK * b2]
            for e in range(1, PAGES_PER_BLOCK):
                t = t + buf_ref[k, PAGES_PER_BLOCK * b2 + e]
            acc = jnp.where(lane == block0 + b2, jnp.sum(t, axis=1, keepdims=True), acc)
        return acc

    return issue, point


def _tail_kernel(*refs, pool_in_kernel, tiles_per_seq, stream, stack, n_steps):
    (x_ref, p_ref, at_ref, sc1_ref, sh1_ref, gt1_ref, sc2_ref, sh2_ref, gt2_ref,
     gpre_ref, gpost_ref, gfpre_ref, gfpost_ref, wg_ref, lin_ref, ps_ref, wbp_ref, wba_ref, wout_ref,
     w1_ref, w2_ref) = refs[:21]
    rest = list(refs[21:])
    if stream is not None:
        pt_ref, ck_ref = rest[:2]
        rest = rest[2:]
    n_groups, per_group = stack
    n_src = n_groups * per_group
    stack_src, rest = rest[:n_src], rest[n_src:]
    o_ref = rest.pop(0)
    if stream is not None:
        km_ref = rest.pop(0)
    stack_dst, rest = rest[:n_groups], rest[n_groups:]
    z_ref = rest.pop(0) if pool_in_kernel else None
    stack_sem = rest.pop() if n_src else None
    stage_ref = rest.pop() if n_src else None
    step = pl.program_id(0)

    def stack_copies(direction):
        copies = []
        for n, s in enumerate(stack_src):
            g, l = divmod(n, per_group)
            rows_per_step = s.shape[0] // n_steps
            rows = pl.ds(pl.multiple_of(step * rows_per_step, 8), rows_per_step)
            if direction == 0:
                copies.append(pltpu.make_async_copy(s.at[rows, :], stage_ref.at[n], stack_sem.at[0, n]))
            else:
                copies.append(pltpu.make_async_copy(stage_ref.at[n], stack_dst[g].at[l, rows, :], stack_sem.at[1, n]))
        return copies

    for cp in stack_copies(0):
        cp.start()

    def stream_point(k):
        pass

    if stream is not None:
        buf_ref, sem_ref, kacc_ref = rest
        issue, point = _cache_stream(pt_ref, ck_ref, buf_ref, sem_ref, **stream)

        @pl.when(step == 0)
        def _():
            for k in range(STREAM_CHUNKS - 1):
                issue(step, k)

        @pl.when((step * stream["pages_per_step"]) % stream["n_pages"] == 0)
        def _():
            kacc_ref[...] = jnp.zeros(kacc_ref.shape, F32)

        def stream_point(k):
            acc = point(step, k, kacc_ref[...])
            kacc_ref[...] = acc
            if k == STREAM_CHUNKS - 1:
                km_ref[0, 0] = acc * (1.0 / MOBA_BLOCK)

    tm = x_ref.shape[0]
    sub = min(tm, SUB_TILE)
    tile_in_seq = step % tiles_per_seq
    if pool_in_kernel:
        _pool_stage(p_ref, z_ref, tile_in_seq)

    def mod_rows(ref, rows):
        return ref[0] if ref.shape[1] == 1 else ref[0, rows, :]

    for s in range(tm // sub):
        rows = slice(s * sub, (s + 1) * sub)
        here = stream_point if s == 0 else (lambda k: None)
        here(0)
        x = x_ref[rows, :]
        h = (_rms(x, gpre_ref[...]) * (1.0 + mod_rows(sc1_ref, rows)) + mod_rows(sh1_ref, rows)).astype(BF16)
        gate_pool = _sigmoid(_bdot(h, wg_ref[:, 0:D_MODEL]))
        gate_attn = _sigmoid(_bdot(h, wg_ref[:, D_MODEL:2 * D_MODEL]))
        if pool_in_kernel:
            d_parts = _pool_rows(z_ref, s * sub, sub, tile_in_seq * tm + s * sub)
        else:
            d_parts = [p_ref[rows, g * POOL_GROUP_WIDTH:(g + 1) * POOL_GROUP_WIDTH] for g in range(len(POOL_WINDOWS))]
        mixed = jnp.concatenate([_bdot(d_parts[g], lin_ref[g]) for g in range(len(POOL_WINDOWS))],
                                axis=-1) * ps_ref[...]
        merged = (gate_pool * _bdot(mixed.astype(BF16), wbp_ref[...])
                  + gate_attn * _bdot(at_ref[rows, :], wba_ref[...]))
        here(1)
        x1 = x + mod_rows(gt1_ref, rows) * _rms(_bdot(merged.astype(BF16), wout_ref[...]), gpost_ref[...])
        h2 = (_rms(x1, gfpre_ref[...]) * (1.0 + mod_rows(sc2_ref, rows)) + mod_rows(sh2_ref, rows)).astype(BF16)
        here(2)
        if s == 0:
            for cp in stack_copies(0):
                cp.wait()
            for cp in stack_copies(1):
                cp.start()
        ff = jnp.zeros(x.shape, F32)
        for c in range(D_FF // D_MODEL):
            cols = slice(c * D_MODEL, (c + 1) * D_MODEL)
            a = jnp.maximum(_bdot(h2, w1_ref[:, cols]), 0.0)
            ff = ff + _bdot((a * a).astype(BF16), w2_ref[cols, :])
            if c == 1:
                here(3)
        o_ref[rows, :] = x1 + mod_rows(gt2_ref, rows) * _rms(ff, gfpost_ref[...])

    for cp in stack_copies(1):
        cp.wait()


def _tail(x, pool_in, attn, mod, gains, weights, *, tm, seq, pool_in_kernel, cache=None, stack=()):
    n_tok = x.shape[0]
    n_steps = n_tok // tm
    tpg = seq // tm
    tok = lambda w: pl.BlockSpec((tm, w), lambda i: (i, 0))
    full = lambda a: pl.BlockSpec(a.shape, lambda i: (0,) * a.ndim, pipeline_mode=pl.Buffered(1))
    mods = [_mod_spec(mod, c, tpg) for c in (1, 0, 2, 4, 3, 5)]
    in_specs = ([tok(D_MODEL), tok(POOL_WIDTH), tok(ATTN_WIDTH)] + mods
                + [full(a) for a in gains] + [full(a) for a in weights])
    operands = [x, pool_in, attn, *([mod] * 6), *gains, *weights]
    out_specs = [tok(D_MODEL)]
    out_shape = [jax.ShapeDtypeStruct((n_tok, D_MODEL), F32)]
    scratch = [pltpu.VMEM((HIST_PAD + tm, POOL_WIDTH), F32)] if pool_in_kernel else []
    stream = None
    if cache is not None:
        page_table, cacheT_k = cache
        depth = cacheT_k.shape[0]
        n_seq, n_pages = page_table.shape
        pages_per_step, rem = divmod(depth * n_seq * n_pages, n_steps)
        assert rem == 0 and n_pages % pages_per_step == 0 and pages_per_step % (STREAM_CHUNKS * PAGES_PER_BLOCK) == 0
        assert n_pages // PAGES_PER_BLOCK <= PAGE_SIZE
        stream = dict(n_seq=n_seq, n_pages=n_pages, pages_per_step=pages_per_step)
        in_specs += [pl.BlockSpec(memory_space=pltpu.SMEM), pl.BlockSpec(memory_space=pl.ANY)]
        operands += [page_table, cacheT_k]
        unit = lambda i: (i * pages_per_step) // n_pages
        out_specs.append(pl.BlockSpec((1, 1, ATTN_WIDTH, PAGE_SIZE), lambda i: (unit(i) // n_seq, unit(i) % n_seq, 0, 0)))
        out_shape.append(jax.ShapeDtypeStruct((depth, n_seq, ATTN_WIDTH, PAGE_SIZE), F32))
        scratch += [pltpu.VMEM((STREAM_CHUNKS, pages_per_step // STREAM_CHUNKS, ATTN_WIDTH, PAGE_SIZE), F32),
                    pltpu.SemaphoreType.DMA((STREAM_CHUNKS,)),
                    pltpu.VMEM((ATTN_WIDTH, PAGE_SIZE), F32)]
    per_group = len(stack[0]) if stack else 0
    if stack:
        hbm = pl.BlockSpec(memory_space=pl.ANY)
        flat = [a for group in stack for a in group]
        assert all(len(g) == per_group for g in stack) and all(a.shape[0] % (8 * n_steps) == 0 for a in flat)
        in_specs += [hbm] * len(flat)
        operands += flat
        out_specs += [hbm] * len(stack)
        out_shape += [jax.ShapeDtypeStruct((per_group,) + g[0].shape, g[0].dtype) for g in stack]
        scratch += [pltpu.VMEM((len(flat), flat[0].shape[0] // n_steps, flat[0].shape[1]), flat[0].dtype),
                    pltpu.SemaphoreType.DMA((2, len(flat)))]
    outs = pl.pallas_call(
        functools.partial(_tail_kernel, pool_in_kernel=pool_in_kernel, tiles_per_seq=tpg, stream=stream,
                          stack=(len(stack), per_group), n_steps=n_steps),
        grid=(n_steps,),
        in_specs=in_specs,
        out_specs=out_specs,
        out_shape=out_shape,
        scratch_shapes=scratch,
        compiler_params=_cparams("arbitrary"),
        name="tail",
    )(*operands)
    return outs


def _select_kernel(q_ref, km_ref, o_ref, *, n_blocks):
    sc = jnp.dot(q_ref[0], km_ref[0, 0], preferred_element_type=F32, precision=lax.Precision.HIGHEST)
    lane = lax.broadcasted_iota(jnp.int32, sc.shape, 1)
    cur = jnp.where(lane < n_blocks, sc, -jnp.inf)
    out = jnp.zeros(sc.shape, jnp.int32)
    for r in range(MOBA_TOPK):
        m = jnp.max(cur, axis=1, keepdims=True)
        pick = jnp.min(jnp.where(cur == m, lane, PAGE_SIZE), axis=1, keepdims=True)
        out = jnp.where(lane == r, pick, out)
        cur = jnp.where(lane == pick, -jnp.inf, cur)
    o_ref[0] = out


def _select_sample(q_exp, kmeanT, layer, n_blocks):
    n_seq, rows, _ = q_exp.shape
    return pl.pallas_call(
        functools.partial(_select_kernel, n_blocks=n_blocks),
        grid=(n_seq,),
        in_specs=[pl.BlockSpec((1, rows, ATTN_WIDTH), lambda b: (b, 0, 0)),
                  pl.BlockSpec((1, 1, ATTN_WIDTH, PAGE_SIZE), lambda b: (layer, b, 0, 0))],
        out_specs=pl.BlockSpec((1, rows, PAGE_SIZE), lambda b: (b, 0, 0)),
        out_shape=jax.ShapeDtypeStruct((n_seq, rows, PAGE_SIZE), jnp.int32),
        compiler_params=_cparams("arbitrary"),
        name="select_sample",
    )(q_exp, kmeanT)


def _attend_kernel(idx_ref, pt_ref, qT_ref, kT_ref, vT_ref, ck_ref, cv_ref, o_ref, kbuf_ref, vbuf_ref, sem_ref,
                   *, layer, n_new):
    n_seq, n_pages = pt_ref.shape
    tiles_per_tok = MOBA_TOPK * PAGES_PER_BLOCK
    tiles_per_head = n_new * tiles_per_tok
    b = pl.program_id(0)
    slot = b % 2

    def tile_copies(hh, phys, sl, j):
        rows = pl.ds(pl.multiple_of(hh * HEAD_DIM, HEAD_DIM), HEAD_DIM)
        return (pltpu.make_async_copy(ck_ref.at[layer, phys, rows, :], kbuf_ref.at[sl, j], sem_ref.at[sl, 0]),
                pltpu.make_async_copy(cv_ref.at[layer, phys, rows, :], vbuf_ref.at[sl, j], sem_ref.at[sl, 1]))

    def issue(bb, sl):
        def per_head(hh, carry):
            base = (bb * N_HEADS + hh) * (n_new * MOBA_TOPK)
            for t in range(n_new):
                for r in range(MOBA_TOPK):
                    blk = idx_ref[base + t * MOBA_TOPK + r]
                    for pg in range(PAGES_PER_BLOCK):
                        logical = jnp.minimum(blk * PAGES_PER_BLOCK + pg, n_pages - 1)
                        j = hh * tiles_per_head + (t * MOBA_TOPK + r) * PAGES_PER_BLOCK + pg
                        ck, cv = tile_copies(hh, pt_ref[bb, logical], sl, j)
                        ck.start()
                        cv.start()
            return carry

        lax.fori_loop(0, N_HEADS, per_head, 0)

    @pl.when(b == 0)
    def _():
        issue(b, slot)
        o_ref[...] = jnp.zeros(o_ref.shape, F32)

    @pl.when(b + 1 < pl.num_programs(0))
    def _():
        issue(b + 1, 1 - slot)

    for j in range(N_HEADS * tiles_per_head):
        ck, cv = tile_copies(0, 0, slot, j)
        ck.wait()
        cv.wait()

    lane = lax.broadcasted_iota(jnp.int32, (1, qT_ref.shape[1]), 1)
    for g0 in range(0, N_HEADS, ATTEND_HEAD_GROUP):
        heads = range(g0, g0 + ATTEND_HEAD_GROUP)
        rows = {hh: slice(hh * HEAD_DIM, (hh + 1) * HEAD_DIM) for hh in heads}
        qT = {hh: qT_ref[rows[hh], :] for hh in heads}
        k_new = {hh: kT_ref[rows[hh], :] for hh in heads}
        v_new = {hh: vT_ref[rows[hh], :] for hh in heads}
        out = {hh: o_ref[rows[hh], :] for hh in heads}
        chains = [(hh, t) for hh in heads for t in range(n_new)]
        col = {c: b * n_new + c[1] for c in chains}
        first = {c: c[0] * tiles_per_head + c[1] * tiles_per_tok for c in chains}
        qcol = {c: jnp.sum(jnp.where(lane == col[c], qT[c[0]], 0.0), axis=1, keepdims=True) * SM_SCALE
                for c in chains}
        scores = {c: [jnp.sum(kbuf_ref[slot, first[c] + j] * qcol[c], axis=0, keepdims=True)
                      for j in range(tiles_per_tok)] for c in chains}
        for c in chains:
            own_ok = (lane >= b * n_new) & (lane <= col[c])
            scores[c].append(jnp.where(own_ok, jnp.sum(k_new[c[0]] * qcol[c], axis=0, keepdims=True), NEG_INF))
        m = {}
        for c in chains:
            mt = scores[c][0]
            for s in scores[c][1:]:
                mt = jnp.maximum(mt, s)
            m[c] = jnp.max(mt, axis=1, keepdims=True)
        probs = {c: [jnp.exp(s - m[c]) for s in scores[c]] for c in chains}
        denom = {}
        for c in chains:
            tot = probs[c][0]
            for p in probs[c][1:]:
                tot = tot + p
            denom[c] = jnp.sum(tot, axis=1, keepdims=True)
        for c in chains:
            pv = v_new[c[0]] * probs[c][-1]
            for j in range(tiles_per_tok):
                pv = pv + vbuf_ref[slot, first[c] + j] * probs[c][j]
            o_col = jnp.sum(pv, axis=1, keepdims=True) / denom[c]
            out[c[0]] = jnp.where(lane == col[c], o_col, out[c[0]])
        for hh in heads:
            o_ref[rows[hh], :] = out[hh]


def _attend_sample(idx_flat, page_table, qT, kT, vT, cacheT_k, cacheT_v, layer, n_new):
    n_seq = page_table.shape[0]
    n_tok = qT.shape[1]
    n_tiles = N_HEADS * n_new * MOBA_TOPK * PAGES_PER_BLOCK
    whole = pl.BlockSpec((ATTN_WIDTH, n_tok), lambda b, idx, pt: (0, 0))
    hbm = pl.BlockSpec(memory_space=pl.ANY)
    return pl.pallas_call(
        functools.partial(_attend_kernel, layer=layer, n_new=n_new),
        grid_spec=pltpu.PrefetchScalarGridSpec(
            num_scalar_prefetch=2,
            grid=(n_seq,),
            in_specs=[whole, whole, whole, hbm, hbm],
            out_specs=whole,
            scratch_shapes=[pltpu.VMEM((2, n_tiles, HEAD_DIM, PAGE_SIZE), F32),
                            pltpu.VMEM((2, n_tiles, HEAD_DIM, PAGE_SIZE), F32),
                            pltpu.SemaphoreType.DMA((2, 2))]),
        out_shape=jax.ShapeDtypeStruct((ATTN_WIDTH, n_tok), F32),
        compiler_params=_cparams("arbitrary"),
        name="attend_sample",
    )(idx_flat, page_table, qT, kT, vT, cacheT_k, cacheT_v)


def kernel(x_prompt, x_sample, cache_k, cache_v, state_pool, page_table, c_prompt, c_sample, w_ada, b_ada, g_mix_pre, g_mix_post, w_in, pool_lin, pool_scale, w_branch_pool, w_branch_attn, w_out, g_ffn_pre, g_ffn_post, w_ff1, w_ff2):
    depth = w_ada.shape[0]
    bp, sp, _ = x_prompt.shape
    bs, ts, _ = x_sample.shape
    n_pages = page_table.shape[1]
    n_blocks = n_pages // PAGES_PER_BLOCK
    n_phys = cache_k.shape[1]

    cacheT_k = jnp.transpose(cache_k, (0, 1, 3, 4, 2)).reshape(depth, n_phys, ATTN_WIDTH, PAGE_SIZE)
    cacheT_v = jnp.transpose(cache_v, (0, 1, 3, 4, 2)).reshape(depth, n_phys, ATTN_WIDTH, PAGE_SIZE)

    ada = _ada(jnp.concatenate([c_prompt, c_sample], axis=0), w_ada, b_ada)

    head_mask = (jnp.arange(ATTN_WIDTH)[None, :] // HEAD_DIM == jnp.arange(N_HEADS)[:, None]).astype(F32)

    xp = x_prompt.reshape(bp * sp, D_MODEL)
    xs = x_sample.reshape(bs * ts, D_MODEL)
    kp_l, vp_l, hp_l, ks_l, vs_l, hs_l = [], [], [], [], [], []
    for l in range(depth):
        wu = w_in[l, :, :POOL_WIDTH].astype(BF16)
        wqkvT = w_in[l, :, POOL_WIDTH:POOL_WIDTH + 3 * ATTN_WIDTH].T.astype(BF16)
        wg = w_in[l, :, POOL_WIDTH + 3 * ATTN_WIDTH:].astype(BF16)
        lin = pool_lin[l].astype(BF16)
        ps = pool_scale[l].reshape(1, POOL_WIDTH)
        wbp = w_branch_pool[l].astype(BF16)
        wba = w_branch_attn[l].astype(BF16)
        wo = w_out[l].astype(BF16)
        w1 = w_ff1[l].astype(BF16)
        w2 = w_ff2[l].astype(BF16)
        gpre = g_mix_pre[l].reshape(1, D_MODEL)
        gpost = g_mix_post[l].reshape(1, D_MODEL)
        gfpre = g_ffn_pre[l].reshape(1, D_MODEL)
        gfpost = g_ffn_post[l].reshape(1, D_MODEL)
        gains = (gpre, gpost, gfpre, gfpost)
        weights = (wg, lin, ps, wbp, wba, wo, w1, w2)
        mod_p = ada[l, :bp].reshape(bp, 1, ADA_CHUNKS * D_MODEL)
        mod_s = jnp.repeat(ada[l, bp:], ts, axis=0).reshape(1, bs * ts, ADA_CHUNKS * D_MODEL)

        u, qT, kT, vT = _inproj(xp, mod_p, gpre, wu, wqkvT, tm=TOKEN_TILE, seq=sp)
        attn = _moba_prompt(qT, kT, vT).reshape(bp * sp, ATTN_WIDTH)
        kp_l.append(kT.reshape(bp * ATTN_WIDTH, sp))
        vp_l.append(vT.reshape(bp * ATTN_WIDTH, sp))
        if l == 0:
            xp, kmeanT = _tail(xp, u, attn, mod_p, gains, weights, tm=STREAM_TILE, seq=sp, pool_in_kernel=True,
                               cache=(page_table, cacheT_k))
        elif l == depth - 1:
            xp, k_all, v_all = _tail(xp, u, attn, mod_p, gains, weights, tm=TOKEN_TILE, seq=sp, pool_in_kernel=True,
                                     stack=(kp_l, vp_l))
        else:
            xp, = _tail(xp, u, attn, mod_p, gains, weights, tm=TOKEN_TILE, seq=sp, pool_in_kernel=True)
        hp_l.append(u.reshape(bp, sp, POOL_WIDTH)[:, sp - POOL_HIST:, :])

        n_s = bs * ts
        u_s, qT_s, kT_s, vT_s = _inproj(xs, mod_s, gpre, wu, wqkvT, tm=n_s, seq=n_s)
        qT_s, kT_s, vT_s = qT_s[0], kT_s[0], vT_s[0]
        z = jnp.concatenate([state_pool[l], u_s.reshape(bs, ts, POOL_WIDTH)], axis=1)
        d_s = _pool_sample(jnp.transpose(z, (1, 0, 2)), ts)
        d_s = jnp.transpose(d_s, (1, 0, 2)).reshape(n_s, POOL_WIDTH)
        q_s = qT_s.T.reshape(bs, 1, ts, ATTN_WIDTH)
        q_exp = (q_s * head_mask[None, :, None, :]).reshape(bs, N_HEADS * ts, ATTN_WIDTH)
        idx = _select_sample(q_exp, kmeanT, l, n_blocks)[:, :, :MOBA_TOPK]
        attnT_s = _attend_sample(idx.reshape(-1), page_table, qT_s, kT_s, vT_s, cacheT_k, cacheT_v, l, ts)
        attn_s = attnT_s.T.astype(BF16)
        xs, = _tail(xs, d_s, attn_s, mod_s, gains, weights, tm=n_s, seq=n_s, pool_in_kernel=False)
        ks_l.append(kT_s.T.reshape(bs, ts, N_HEADS, HEAD_DIM))
        vs_l.append(vT_s.T.reshape(bs, ts, N_HEADS, HEAD_DIM))
        hs_l.append(z[:, ts:, :])

    def untranspose(stacked):
        return jnp.transpose(stacked.reshape(depth, bp, N_HEADS, HEAD_DIM, sp), (0, 1, 4, 2, 3))

    return (xp.reshape(bp, sp, D_MODEL), xs.reshape(bs, ts, D_MODEL),
            untranspose(k_all), untranspose(v_all), jnp.stack(hp_l),
            jnp.stack(ks_l), jnp.stack(vs_l), jnp.stack(hs_l))
```
